```python
import math
import jax, jax.numpy as jnp
from jax import lax
import numpy as np

D_MODEL = 1024
BATCH = 8
SEQ = 2048
DEPTH = 4

CTX_LEN = 256
GRID_W = 64

W_LRU = D_MODEL // 2
W_POOL = D_MODEL // 4
W_CONF = D_MODEL // 4
W_MIX = W_LRU + W_POOL + W_CONF

LRU_HEADS = 8
LRU_HD = W_LRU // LRU_HEADS
LRU_CONV = 4
LRU_C = 8.0

POOL_WINDOWS = (2, 4, 8, 16)
POOL_GROUPS = len(POOL_WINDOWS)
POOL_GD = W_POOL // POOL_GROUPS

CONF_K = 31

IN_WIDTHS = (W_LRU, W_LRU, W_POOL, W_POOL, W_CONF, W_CONF, W_CONF)
N_IN = sum(IN_WIDTHS)
IN_SPLITS = tuple(int(v) for v in np.cumsum(IN_WIDTHS)[:-1])

EPS = 1e-6
LN_EPS = 1e-5

kernel_name = "hybrid_lru_pool_conformer_dit"


def rmsnorm(x, g):
    x32 = x.astype(jnp.float32)
    y = x32 * lax.rsqrt(jnp.mean(x32 * x32, axis=-1, keepdims=True) + EPS)
    return (y * g.astype(jnp.float32)).astype(x.dtype)


def pos_embed_2d(n):
    ROWS = n // GRID_W
    row = jnp.repeat(jnp.arange(ROWS, dtype=jnp.float32), GRID_W)
    col = jnp.tile(jnp.arange(GRID_W, dtype=jnp.float32), ROWS)
    q = D_MODEL // 4
    omega = 1.0 / (10000.0 ** (jnp.arange(q, dtype=jnp.float32) / q))
    def enc(p):
        ang = p[:, None] * omega[None, :]
        return jnp.concatenate([jnp.sin(ang), jnp.cos(ang)], axis=-1)
    return jnp.concatenate([enc(row), enc(col)], axis=-1)


def modulation(cond, w_mod, b_mod):
    m = jnp.dot(jax.nn.silu(cond), w_mod) + b_mod
    return jnp.split(m, 3, axis=-1)


def dwconv(x, w, b, pad_l, pad_r):
    C = x.shape[-1]
    y = lax.conv_general_dilated(
        x, w[:, None, :].astype(x.dtype), window_strides=(1,), padding=[(pad_l, pad_r)],
        dimension_numbers=('NWC', 'WIO', 'NWC'), feature_group_count=C)
    return y + b


def lru_coeffs(xc, wa, ba, wx, bx, lam):
    B_, N, _ = xc.shape
    xh = xc.reshape(B_, N, LRU_HEADS, LRU_HD)
    r = jax.nn.sigmoid(jnp.einsum('bnhi,hij->bnhj', xh, wa).reshape(B_, N, W_LRU) + ba)
    i = jax.nn.sigmoid(jnp.einsum('bnhi,hij->bnhj', xh, wx).reshape(B_, N, W_LRU) + bx)
    log_a = (-LRU_C * r.astype(jnp.float32)) * jax.nn.softplus(-lam.astype(jnp.float32))
    a = jnp.exp(log_a)
    mult = jnp.sqrt(-jnp.expm1(2.0 * log_a))
    b = mult * (i * xc).astype(jnp.float32)
    return a, b


def _combine(left, right):
    a_l, b_l = left
    a_r, b_r = right
    return a_l * a_r, a_r * b_l + b_r


def linear_scan(a, b, h0, reverse):
    if h0 is not None:
        idx = -1 if reverse else 0
        b = b.at[:, idx].add(a[:, idx] * h0)
    _, h = lax.associative_scan(_combine, (a, b), reverse=reverse, axis=1)
    return h


def lru_branch(vx, vc, conv_w, conv_b, wa, ba, wx, bx, lam):
    pl = LRU_CONV // 2
    pr = LRU_CONV - 1 - pl
    xc = dwconv(vx, conv_w, conv_b, pl, pr)
    cc = dwconv(vc, conv_w, conv_b, pl, pr)
    y_x = jnp.zeros(vx.shape, jnp.float32)
    y_c = jnp.zeros(vc.shape, jnp.float32)
    for d, rev in enumerate((False, True)):
        a_c, b_c = lru_coeffs(cc, wa[d], ba[d], wx[d], bx[d], lam[d])
        h_c = linear_scan(a_c, b_c, None, rev)
        h0 = h_c[:, 0] if rev else h_c[:, -1]
        a_x, b_x = lru_coeffs(xc, wa[d], ba[d], wx[d], bx[d], lam[d])
        h_x = linear_scan(a_x, b_x, h0, rev)
        y_x = y_x + h_x
        y_c = y_c + h_c
    return y_x.astype(vx.dtype), y_c.astype(vc.dtype)


def pool_mixer(v, pool_w, pool_scale):
    B_, N, _ = v.shape
    v32 = v.astype(jnp.float32)
    cs = jnp.concatenate([jnp.zeros((B_, 1, W_POOL), jnp.float32), jnp.cumsum(v32, axis=1)], axis=1)
    t = jnp.arange(N)
    outs = []
    for gi, w in enumerate(POOL_WINDOWS):
        lo = jnp.clip(t - w // 2, 0, N)
        hi = jnp.clip(t + w // 2, 0, N)
        sl = cs[..., gi * POOL_GD:(gi + 1) * POOL_GD]
        mean = (jnp.take(sl, hi, axis=1) - jnp.take(sl, lo, axis=1)) / (hi - lo).astype(jnp.float32)[:, None]
        outs.append(mean - v32[..., gi * POOL_GD:(gi + 1) * POOL_GD])
    d = jnp.stack(outs, axis=2).astype(v.dtype)
    y = jnp.einsum('bngi,gij->bngj', d, pool_w).reshape(B_, N, W_POOL)
    return y * pool_scale


def conformer_conv(v, gg, dw, dw_b, ln_g, ln_b, pw):
    z = v * jax.nn.sigmoid(gg)
    z = dwconv(z, dw, dw_b, CONF_K // 2, CONF_K // 2)
    z32 = z.astype(jnp.float32)
    mu = jnp.mean(z32, axis=-1, keepdims=True)
    var = jnp.mean(jnp.square(z32 - mu), axis=-1, keepdims=True)
    z = ((z32 - mu) * lax.rsqrt(var + LN_EPS) * ln_g.astype(jnp.float32) + ln_b.astype(jnp.float32)).astype(v.dtype)
    z = jax.nn.silu(z)
    return jnp.einsum('bnc,ce->bne', z, pw)


def merge_out(y_lru, g_lru, y_pool, g_pool, y_conf, g_conf, w_out):
    z = jnp.concatenate([y_lru * jax.nn.silu(g_lru), y_pool * jax.nn.silu(g_pool),
                         y_conf * jax.nn.silu(g_conf)], axis=-1)
    return jnp.einsum('bnm,md->bnd', z, w_out)


def setup_inputs(seed: int = 0) -> dict:
    key = jax.random.key(seed)
    ks = jax.random.split(key, 24)
    f32 = jnp.float32
    def nrm(k, shape, s):
        return jax.random.normal(k, shape, f32) * s
    L = DEPTH
    x = nrm(ks[0], (BATCH, SEQ, D_MODEL), 1.0)
    c = nrm(ks[1], (BATCH, D_MODEL), 1.0)
    ctx = nrm(ks[2], (BATCH, CTX_LEN, D_MODEL), 1.0)
    c_ctx = nrm(ks[3], (D_MODEL,), 1.0)
    w_mod = nrm(ks[4], (L, D_MODEL, 3 * D_MODEL), 0.5 * D_MODEL ** -0.5)
    b_mod = nrm(ks[5], (L, 3 * D_MODEL), 0.02)
    g_pre = 1.0 + nrm(ks[6], (L, D_MODEL), 0.05)
    w_in = nrm(ks[7], (L, D_MODEL, N_IN), D_MODEL ** -0.5)
    lru_conv_w = nrm(ks[8], (L, LRU_CONV, W_LRU), LRU_CONV ** -0.5)
    lru_conv_b = nrm(ks[9], (L, W_LRU), 0.02)
    lru_wa = nrm(ks[10], (L, 2, LRU_HEADS, LRU_HD, LRU_HD), LRU_HD ** -0.5)
    lru_ba = nrm(ks[11], (L, 2, W_LRU), 0.02)
    lru_wx = nrm(ks[12], (L, 2, LRU_HEADS, LRU_HD, LRU_HD), LRU_HD ** -0.5)
    lru_bx = nrm(ks[13], (L, 2, W_LRU), 0.02)
    a_pow = jax.random.uniform(ks[14], (L, 2, W_LRU), f32, 0.9, 0.999)
    s = a_pow ** (1.0 / LRU_C)
    lru_lambda = jnp.log(s) - jnp.log1p(-s)
    pool_w = nrm(ks[15], (L, POOL_GROUPS, POOL_GD, POOL_GD), POOL_GD ** -0.5)
    pool_scale = 1.0 + nrm(ks[16], (L, W_POOL), 0.1)
    conf_dw = nrm(ks[17], (L, CONF_K, W_CONF), CONF_K ** -0.5)
    conf_dw_b = nrm(ks[18], (L, W_CONF), 0.02)
    conf_ln_g = 1.0 + nrm(ks[19], (L, W_CONF), 0.05)
    conf_ln_b = nrm(ks[20], (L, W_CONF), 0.02)
    conf_pw = nrm(ks[21], (L, W_CONF, W_CONF), W_CONF ** -0.5)
    w_out = nrm(ks[22], (L, W_MIX, D_MODEL), W_MIX ** -0.5)
    g_post = 1.0 + nrm(ks[23], (L, D_MODEL), 0.05)
    return {"x": x, "c": c, "ctx": ctx, "c_ctx": c_ctx, "w_mod": w_mod, "b_mod": b_mod,
            "g_pre": g_pre, "w_in": w_in, "lru_conv_w": lru_conv_w, "lru_conv_b": lru_conv_b,
            "lru_wa": lru_wa, "lru_ba": lru_ba, "lru_wx": lru_wx, "lru_bx": lru_bx,
            "lru_lambda": lru_lambda, "pool_w": pool_w, "pool_scale": pool_scale,
            "conf_dw": conf_dw, "conf_dw_b": conf_dw_b, "conf_ln_g": conf_ln_g, "conf_ln_b": conf_ln_b,
            "conf_pw": conf_pw, "w_out": w_out, "g_post": g_post}


def reference(x, c, ctx, c_ctx, w_mod, b_mod, g_pre, w_in, lru_conv_w, lru_conv_b, lru_wa, lru_ba,
              lru_wx, lru_bx, lru_lambda, pool_w, pool_scale, conf_dw, conf_dw_b, conf_ln_g, conf_ln_b,
              conf_pw, w_out, g_post):
    n = x.shape[1]
    x = x + pos_embed_2d(n).astype(x.dtype)[None]
    for l in range(DEPTH):
        need_ctx = l < DEPTH - 1
        sh_x, sc_x, gt_x = modulation(c, w_mod[l], b_mod[l])
        sh_c, sc_c, gt_c = modulation(c_ctx, w_mod[l], b_mod[l])
        hx = rmsnorm(x, g_pre[l]) * (1 + sc_x[:, None, :]) + sh_x[:, None, :]
        hc = rmsnorm(ctx, g_pre[l]) * (1 + sc_c) + sh_c
        ux = jnp.einsum('bnd,de->bne', hx, w_in[l])
        vx_lru, gx_lru, vx_pool, gx_pool, vx_conf, ggx_conf, gx_conf = jnp.split(ux, IN_SPLITS, axis=-1)
        if need_ctx:
            uc = jnp.einsum('bnd,de->bne', hc, w_in[l])
            vc_lru, gc_lru, vc_pool, gc_pool, vc_conf, ggc_conf, gc_conf = jnp.split(uc, IN_SPLITS, axis=-1)
        else:
            vc_lru = jnp.einsum('bnd,de->bne', hc, w_in[l][:, :W_LRU])

        y_lru_x, y_lru_c = lru_branch(vx_lru, vc_lru, lru_conv_w[l], lru_conv_b[l], lru_wa[l], lru_ba[l],
                                      lru_wx[l], lru_bx[l], lru_lambda[l])
        y_pool_x = pool_mixer(vx_pool, pool_w[l], pool_scale[l])
        y_conf_x = conformer_conv(vx_conf, ggx_conf, conf_dw[l], conf_dw_b[l], conf_ln_g[l], conf_ln_b[l], conf_pw[l])
        o_x = merge_out(y_lru_x, gx_lru, y_pool_x, gx_pool, y_conf_x, gx_conf, w_out[l])

        if need_ctx:
            y_pool_c = pool_mixer(vc_pool, pool_w[l], pool_scale[l])
            y_conf_c = conformer_conv(vc_conf, ggc_conf, conf_dw[l], conf_dw_b[l], conf_ln_g[l], conf_ln_b[l], conf_pw[l])
            o_c = merge_out(y_lru_c, gc_lru, y_pool_c, gc_pool, y_conf_c, gc_conf, w_out[l])
            ctx = ctx + gt_c * rmsnorm(o_c, g_post[l])

        x = x + gt_x[:, None, :] * rmsnorm(o_x, g_post[l])
    return x
```

```python
import functools

import jax
import jax.numpy as jnp
import numpy as np
from jax import lax
from jax.experimental import pallas as pl
from jax.experimental.pallas import tpu as pltpu

D = 1024
NB = 8
N_LAT = 2048
N_CTX = 256
DEPTH = 4
GRID_W = 64
W_LRU, W_POOL, W_CONF = 512, 256, 256
N_VAL = W_LRU + W_POOL + 2 * W_CONF
N_IN = N_VAL + D
HEADS, HD = 8, 64
LRU_K = 4
LRU_C = 8.0
CONF_K = 31
EPS = 1e-6
LN_EPS = 1e-5

T = 128
H = 16
HP = 2
TS = 8
RB = 256
TB = RB // NB
NCH_C = N_CTX // T
NCH_L = N_LAT // T
NCH = NCH_C + NCH_L
NT = N_CTX + N_LAT
VMEM_LIMIT = 58 * 1024 * 1024

C_POOL = W_LRU
C_CONF = W_LRU + W_POOL
C_GG = W_LRU + W_POOL + W_CONF

f32 = jnp.float32
bf16 = jnp.bfloat16


def _silu(g):
    return g * jax.nn.sigmoid(g)


def _norm_rows(x, gs, sh):
    ms = jnp.mean(x * x, axis=-1, keepdims=True)
    return (x * lax.rsqrt(ms + EPS)) * gs + sh


def _fill_hx(src_ref, n_steps, ts, dst_row0, valid, gs, sh, hx_scr):
    def body(j, c):
        t0 = pl.multiple_of(j * ts, ts)
        hx = _norm_rows(src_ref[pl.ds(t0, ts)], gs, sh)
        if valid is not None:
            hx = jnp.where(valid, hx, 0.0)
        r0 = pl.multiple_of(dst_row0 + t0 * NB, 16)
        hx_scr[pl.ds(r0, ts * NB), :] = hx.reshape(ts * NB, D).astype(bf16)
        return c
    lax.fori_loop(0, n_steps // ts, body, 0)


def _chunk_flags(c):
    is_lat = c >= NCH_C
    first = jnp.logical_or(c == 0, c == NCH_C)
    last = jnp.logical_or(c == NCH_C - 1, c == NCH - 1)
    return is_lat, first, last


def _mod_rows(mod_ref, gpre_ref, is_lat):
    mod = mod_ref[is_lat.astype(jnp.int32)]
    gs = (gpre_ref[...] * (1.0 + mod[:, D:2 * D]))[None]
    sh = mod[:, 0:D][None]
    gt = mod[:, 2 * D:3 * D][None]
    return gs, sh, gt


def _lru_coeffs(xc, wa_ref, ba_ref, bx_ref, c8):
    n = xc.shape[0]
    x2 = xc.reshape(n * NB, W_LRU)
    xb = x2.astype(bf16)
    a_parts, b_parts = [], []
    for kt in range(2):
        sl = slice(256 * kt, 256 * (kt + 1))
        r = jax.nn.sigmoid(jnp.dot(xb[:, sl], wa_ref[0, kt], preferred_element_type=f32) + ba_ref[:, sl])
        ig = jax.nn.sigmoid(jnp.dot(xb[:, sl], wa_ref[1, kt], preferred_element_type=f32) + bx_ref[:, sl])
        la = r * c8[:, sl]
        a = jnp.exp(la)
        mult = jnp.sqrt(-jnp.tanh(la) * (1.0 + a * a))
        b = mult * (ig * x2[:, sl])
        a_parts.append(a.reshape(n, NB, 256))
        b_parts.append(b.reshape(n, NB, 256))
    return a_parts, b_parts


def _softplus(y):
    return jnp.maximum(y, 0.0) + jnp.log1p(jnp.exp(-jnp.abs(y)))


def _pre_chunk(j):
    return jnp.where(j < NCH_C, NCH_C - 1 - j, NCH - 1 - (j - NCH_C))


def _pre_kernel(xm_ref, xb_ref, xa_ref, mod_ref, gpre_ref, w_ref, lcw_ref, lcb_ref, wa_ref, ba_ref,
                bx_ref, lam_ref, hb_ref, hx_scr, v_scr, xc_scr, a_scr, h_scr):
    j = pl.program_id(0)
    is_lat, first, last = _chunk_flags(_pre_chunk(j))
    gs, sh, _ = _mod_rows(mod_ref, gpre_ref, is_lat)

    @pl.when(j == 0)
    def _():
        h_scr[...] = jnp.zeros_like(h_scr)

    _fill_hx(xb_ref, HP, HP, 0, jnp.logical_not(first), gs, sh, hx_scr)
    _fill_hx(xm_ref, T, TS, HP * NB, None, gs, sh, hx_scr)
    _fill_hx(xa_ref, HP, HP, (HP + T) * NB, jnp.logical_not(last), gs, sh, hx_scr)

    half = (T + 2 * HP) // 2
    for p in range(2):
        v = jnp.dot(hx_scr[p * half * NB:(p + 1) * half * NB, :], w_ref[...], preferred_element_type=f32)
        v_scr[p * half:(p + 1) * half] = v.reshape(half, NB, W_LRU)

    def conv_body(i, c):
        t0 = pl.multiple_of(i * TS, TS)
        acc = jnp.zeros((TS, NB, W_LRU), f32) + lcb_ref[...][None]
        for k in range(LRU_K):
            acc = acc + v_scr[pl.ds(t0 + k, TS)] * lcw_ref[k:k + 1, :][None]
        xc_scr[pl.ds(t0, TS)] = acc
        return c
    lax.fori_loop(0, T // TS, conv_body, 0)

    c8 = -LRU_C * _softplus(-lam_ref[...])

    def coef_body(i, c):
        t0 = pl.multiple_of(i * 16, 16)
        a_parts, b_parts = _lru_coeffs(xc_scr[pl.ds(t0, 16)], wa_ref, ba_ref, bx_ref, c8)
        for kt in range(2):
            a_scr[pl.ds(t0, 16), :, 256 * kt:256 * (kt + 1)] = a_parts[kt]
            xc_scr[pl.ds(t0, 16), :, 256 * kt:256 * (kt + 1)] = b_parts[kt]
        return c
    lax.fori_loop(0, T // 16, coef_body, 0)

    def scan_body(s, h):
        t = T - 1 - s
        h = a_scr[t] * h + xc_scr[t]
        hb_ref[t] = h
        return h
    h_scr[...] = lax.fori_loop(0, T, scan_body, h_scr[...], unroll=8)


def _main_kernel(xm_ref, xb_ref, xa_ref, hb_ref, mod_ref, gpre_ref, w_ref, lcw_ref, lcb_ref, wa_ref,
                 ba_ref, bx_ref, lam_ref, poolw_ref, pscale_ref, cdw_ref, cdb_ref, lng_ref, lnb_ref,
                 pw_ref, wout_ref, gpost_ref, out_ref,
                 hx_scr, v_scr, g_scr, xc_scr, zc_scr, d_scr, c_scr, z_scr, h_scr):
    i = pl.program_id(0)
    is_lat, first, last = _chunk_flags(i)
    gs, sh, gt = _mod_rows(mod_ref, gpre_ref, is_lat)

    @pl.when(i == 0)
    def _():
        h_scr[...] = jnp.zeros_like(h_scr)

    _fill_hx(xb_ref, H, TS, 0, jnp.logical_not(first), gs, sh, hx_scr)
    _fill_hx(xm_ref, T, TS, H * NB, None, gs, sh, hx_scr)
    _fill_hx(xa_ref, H, TS, (H + T) * NB, jnp.logical_not(last), gs, sh, hx_scr)

    for t_off in (0, H + T):
        v = jnp.dot(hx_scr[t_off * NB:(t_off + H) * NB, :], w_ref[:, 0:N_VAL], preferred_element_type=f32)
        v_scr[t_off:t_off + H] = v.reshape(H, NB, N_VAL)

    def proj_body(jb, c):
        r0 = pl.multiple_of(H * NB + jb * RB, RB // 2)
        t0 = pl.multiple_of(jb * TB, TB)
        u = jnp.dot(hx_scr[pl.ds(r0, RB), :], w_ref[...], preferred_element_type=f32)
        v_scr[pl.ds(H + t0, TB)] = u[:, 0:N_VAL].reshape(TB, NB, N_VAL)
        g_scr[pl.ds(t0, TB)] = u[:, N_VAL:N_IN].reshape(TB, NB, D)
        return c
    lax.fori_loop(0, T // TB, proj_body, 0)

    def conv4_body(jb, c):
        t0 = pl.multiple_of(jb * TS, TS)
        acc = jnp.zeros((TS, NB, W_LRU), f32) + lcb_ref[...][None]
        for k in range(LRU_K):
            acc = acc + v_scr[pl.ds(H + t0 + k - LRU_K // 2, TS), :, 0:W_LRU] * lcw_ref[k:k + 1, :][None]
        xc_scr[pl.ds(t0, TS)] = acc
        return c
    lax.fori_loop(0, T // TS, conv4_body, 0)

    c8 = -LRU_C * _softplus(-lam_ref[...])

    def coef_body(jb, c):
        t0 = pl.multiple_of(jb * 16, 16)
        a_parts, b_parts = _lru_coeffs(xc_scr[pl.ds(t0, 16)], wa_ref, ba_ref, bx_ref, c8)
        for kt in range(2):
            v_scr[pl.ds(H + t0, 16), :, 256 * kt:256 * (kt + 1)] = a_parts[kt]
            xc_scr[pl.ds(t0, 16), :, 256 * kt:256 * (kt + 1)] = b_parts[kt]
        return c
    lax.fori_loop(0, T // 16, coef_body, 0)

    def scan_body(t, h):
        h = v_scr[H + t, :, 0:W_LRU] * h + xc_scr[t]
        xc_scr[t] = h
        return h
    h_scr[...] = lax.fori_loop(0, T, scan_body, h_scr[...], unroll=8)

    def lru_out_body(jb, c):
        t0 = pl.multiple_of(jb * TS, TS)
        y = xc_scr[pl.ds(t0, TS)] + hb_ref[pl.ds(t0, TS)]
        zz = y * _silu(g_scr[pl.ds(t0, TS), :, 0:W_LRU])
        r0 = pl.multiple_of(t0 * NB, TS * NB)
        z_scr[pl.ds(r0, TS * NB), 0:W_LRU] = zz.reshape(TS * NB, W_LRU).astype(bf16)
        return c
    lax.fori_loop(0, T // TS, lru_out_body, 0)

    nseq = jnp.where(is_lat, N_LAT, N_CTX)
    tseq0 = (i - jnp.where(is_lat, NCH_C, 0)) * T
    hi_half = lax.broadcasted_iota(jnp.int32, (1, 1, 128), 2) >= 64
    hw_a = jnp.where(hi_half, 2, 1)
    hw_b = jnp.where(hi_half, 8, 4)

    def pool_body(jb, c):
        t0 = pl.multiple_of(jb * TS, TS)
        base = H + t0
        tg = lax.broadcasted_iota(jnp.int32, (TS, NB, 128), 0) + (tseq0 + t0)

        def count(hw):
            return (jnp.minimum(tg + hw, nseq) - jnp.maximum(tg - hw, 0)).astype(f32)

        def ld(k, c0):
            return v_scr[pl.ds(base + k, TS), :, c0:c0 + 128]

        ca, cb = C_POOL, C_POOL + 128
        va = ld(0, ca)
        sa = (ld(-1, ca) + va) + jnp.where(hi_half, ld(-2, ca) + ld(1, ca), 0.0)
        vb = ld(0, cb)
        inner = vb
        for k in (-4, -3, -2, -1, 1, 2, 3):
            inner = inner + ld(k, cb)
        outer = ld(-8, cb)
        for k in (-7, -6, -5, 4, 5, 6, 7):
            outer = outer + ld(k, cb)
        sb = inner + jnp.where(hi_half, outer, 0.0)
        da = sa / count(hw_a) - va
        db = sb / count(hw_b) - vb
        r0 = pl.multiple_of(t0 * NB, TS * NB)
        d_scr[pl.ds(r0, TS * NB), 0:128] = da.reshape(TS * NB, 128).astype(bf16)
        d_scr[pl.ds(r0, TS * NB), 128:256] = db.reshape(TS * NB, 128).astype(bf16)
        return c
    lax.fori_loop(0, T // TS, pool_body, 0)

    def glu_body(jb, c):
        t0 = pl.multiple_of(jb * TS, TS)
        v = v_scr[pl.ds(t0, TS), :, C_CONF:C_CONF + W_CONF]
        gg = v_scr[pl.ds(t0, TS), :, C_GG:C_GG + W_CONF]
        zc_scr[pl.ds(t0, TS)] = v * jax.nn.sigmoid(gg)
        return c
    lax.fori_loop(0, (T + 2 * H) // TS, glu_body, 0)

    def conf_body(jb, c):
        t0 = pl.multiple_of(jb * TS, TS)
        acc = jnp.zeros((TS, NB, W_CONF), f32) + cdb_ref[...][None]
        for k in range(CONF_K):
            acc = acc + zc_scr[pl.ds(t0 + H - CONF_K // 2 + k, TS)] * cdw_ref[k:k + 1, :][None]
        mu = jnp.mean(acc, axis=-1, keepdims=True)
        zc = acc - mu
        var = jnp.mean(zc * zc, axis=-1, keepdims=True)
        zn = zc * lax.rsqrt(var + LN_EPS) * lng_ref[...][None] + lnb_ref[...][None]
        r0 = pl.multiple_of(t0 * NB, TS * NB)
        c_scr[pl.ds(r0, TS * NB), :] = _silu(zn).reshape(TS * NB, W_CONF).astype(bf16)
        return c
    lax.fori_loop(0, T // TS, conf_body, 0)

    def branch_out_body(jb, c):
        r0 = pl.multiple_of(jb * RB, RB)
        t0 = pl.multiple_of(jb * TB, TB)
        yp = jnp.dot(d_scr[pl.ds(r0, RB), :], poolw_ref[...], preferred_element_type=f32) * pscale_ref[...]
        yc = jnp.dot(c_scr[pl.ds(r0, RB), :], pw_ref[...], preferred_element_type=f32)
        gp = g_scr[pl.ds(t0, TB), :, W_LRU:W_LRU + W_POOL].reshape(RB, W_POOL)
        gc = g_scr[pl.ds(t0, TB), :, W_LRU + W_POOL:D].reshape(RB, W_CONF)
        z_scr[pl.ds(r0, RB), W_LRU:W_LRU + W_POOL] = (yp * _silu(gp)).astype(bf16)
        z_scr[pl.ds(r0, RB), W_LRU + W_POOL:D] = (yc * _silu(gc)).astype(bf16)
        return c
    lax.fori_loop(0, T // TB, branch_out_body, 0)

    def out_body(jb, c):
        r0 = pl.multiple_of(jb * RB, RB)
        t0 = pl.multiple_of(jb * TB, TB)
        o = jnp.dot(z_scr[pl.ds(r0, RB), :], wout_ref[...], preferred_element_type=f32)
        ms = jnp.mean(o * o, axis=-1, keepdims=True)
        y = (o * lax.rsqrt(ms + EPS)) * gpost_ref[...]
        out_ref[pl.ds(t0, TB)] = xm_ref[pl.ds(t0, TB)] + gt * y.reshape(TB, NB, D)
        return c
    lax.fori_loop(0, T // TB, out_body, 0)


def _mod_kernel(cond_ref, w_ref, b_ref, out_ref):
    s = _silu(cond_ref[...])
    out_ref[...] = jnp.dot(s, w_ref[...], preferred_element_type=f32,
                           precision=lax.Precision.HIGHEST) + b_ref[...]


def _pos_kernel(x_ref, pos_ref, out_ref):
    out_ref[...] = x_ref[...] + pos_ref[...]


def _pos_embed(n):
    rows = n // GRID_W
    row = jnp.repeat(jnp.arange(rows, dtype=f32), GRID_W)
    col = jnp.tile(jnp.arange(GRID_W, dtype=f32), rows)
    q = D // 4
    omega = 1.0 / (10000.0 ** (jnp.arange(q, dtype=f32) / q))

    def enc(p):
        ang = p[:, None] * omega[None, :]
        return jnp.concatenate([jnp.sin(ang), jnp.cos(ang)], axis=-1)
    return jnp.concatenate([enc(row), enc(col)], axis=-1)


def _block_diag(w, per_tile):
    *lead, g, n, _ = w.shape
    w = w.reshape(*lead, g // per_tile, per_tile, n, n)
    eye = jnp.eye(per_tile, dtype=w.dtype)
    out = jnp.einsum('...jio,jm->...jimo', w, eye)
    return out.reshape(*lead, g // per_tile, per_tile * n, per_tile * n)


def _const_spec(block, index, single=True):
    return pl.BlockSpec(block, lambda i: index, pipeline_mode=pl.Buffered(1) if single else None)


def kernel(x, c, ctx, c_ctx, w_mod, b_mod, g_pre, w_in, lru_conv_w, lru_conv_b, lru_wa, lru_ba, lru_wx,
           lru_bx, lru_lambda, pool_w, pool_scale, conf_dw, conf_dw_b, conf_ln_g, conf_ln_b, conf_pw,
           w_out, g_post):
    L = DEPTH
    perm = np.concatenate([np.arange(0, 512), np.arange(1024, 1280), np.arange(1536, 2048),
                           np.arange(512, 1024), np.arange(1280, 1536), np.arange(2048, 2304)])
    w_all = w_in[:, :, perm].astype(bf16)
    wa_t = jnp.stack([_block_diag(lru_wa, 4), _block_diag(lru_wx, 4)], axis=2).astype(bf16)
    poolw_t = _block_diag(pool_w, 4)[:, 0].astype(bf16)
    pw_b = conf_pw.astype(bf16)
    wout_b = w_out.astype(bf16)
    row = lambda a: a[:, None, :]
    dir_row = lambda a: a[:, :, None, :]

    cond = jnp.concatenate([jnp.broadcast_to(c_ctx[None, :], (NB, D)), c], axis=0)
    tn = 1024
    mod = pl.pallas_call(
        _mod_kernel,
        out_shape=jax.ShapeDtypeStruct((L, 2 * NB, 3 * D), f32),
        grid=(L, 3 * D // tn),
        in_specs=[pl.BlockSpec((2 * NB, D), lambda l, n: (0, 0)),
                  pl.BlockSpec((None, D, tn), lambda l, n: (l, 0, n)),
                  pl.BlockSpec((None, 1, tn), lambda l, n: (l, 0, n))],
        out_specs=pl.BlockSpec((None, 2 * NB, tn), lambda l, n: (l, 0, n)),
        name="modulation",
    )(cond, w_mod, row(b_mod))
    mod = mod.reshape(L, 2, NB, 3 * D)

    tp = 512
    x0 = pl.pallas_call(
        _pos_kernel,
        out_shape=jax.ShapeDtypeStruct(x.shape, f32),
        grid=(NB, N_LAT // tp),
        in_specs=[pl.BlockSpec((None, tp, D), lambda b, n: (b, n, 0)),
                  pl.BlockSpec((tp, D), lambda b, n: (n, 0))],
        out_specs=pl.BlockSpec((None, tp, D), lambda b, n: (b, n, 0)),
        name="pos_add",
    )(x, _pos_embed(N_LAT))
    s = jnp.concatenate([jnp.transpose(ctx, (1, 0, 2)), jnp.transpose(x0, (1, 0, 2))], axis=0)

    cparams = pltpu.CompilerParams(dimension_semantics=("arbitrary",), vmem_limit_bytes=VMEM_LIMIT)

    for l in range(L):
        lru_common = [
            _const_spec((None, 2, NB, 3 * D), (l, 0, 0, 0)),
            _const_spec((None, 1, D), (l, 0, 0)),
        ]
        pre_in_specs = [
            pl.BlockSpec((T, NB, D), lambda j: (_pre_chunk(j), 0, 0)),
            pl.BlockSpec((HP, NB, D), lambda j: (jnp.maximum(_pre_chunk(j) * (T // HP) - 1, 0), 0, 0)),
            pl.BlockSpec((HP, NB, D), lambda j: (jnp.minimum((_pre_chunk(j) + 1) * (T // HP), NT // HP - 1), 0, 0)),
            *lru_common,
            _const_spec((None, D, W_LRU), (l, 0, 0)),
            _const_spec((None, LRU_K, W_LRU), (l, 0, 0)),
            _const_spec((None, 1, W_LRU), (l, 0, 0)),
            _const_spec((None, None, 2, 2, 256, 256), (l, 1, 0, 0, 0, 0)),
            _const_spec((None, None, 1, W_LRU), (l, 1, 0, 0)),
            _const_spec((None, None, 1, W_LRU), (l, 1, 0, 0)),
            _const_spec((None, None, 1, W_LRU), (l, 1, 0, 0)),
        ]
        hb = pl.pallas_call(
            _pre_kernel,
            out_shape=jax.ShapeDtypeStruct((NT, NB, W_LRU), f32),
            grid=(NCH,),
            in_specs=pre_in_specs,
            out_specs=pl.BlockSpec((T, NB, W_LRU), lambda j: (_pre_chunk(j), 0, 0)),
            scratch_shapes=[
                pltpu.VMEM(((T + 2 * HP) * NB, D), bf16),
                pltpu.VMEM((T + 2 * HP, NB, W_LRU), f32),
                pltpu.VMEM((T, NB, W_LRU), f32),
                pltpu.VMEM((T, NB, W_LRU), f32),
                pltpu.VMEM((NB, W_LRU), f32),
            ],
            compiler_params=cparams,
            name=f"lru_bwd_{l}",
        )(s, s, s, mod, row(g_pre), w_all, lru_conv_w, row(lru_conv_b), wa_t, dir_row(lru_ba),
          dir_row(lru_bx), dir_row(lru_lambda))

        main_in_specs = [
            pl.BlockSpec((T, NB, D), lambda i: (i, 0, 0)),
            pl.BlockSpec((H, NB, D), lambda i: (jnp.maximum(i * (T // H) - 1, 0), 0, 0)),
            pl.BlockSpec((H, NB, D), lambda i: (jnp.minimum((i + 1) * (T // H), NT // H - 1), 0, 0)),
            pl.BlockSpec((T, NB, W_LRU), lambda i: (i, 0, 0)),
            *lru_common,
            _const_spec((None, D, N_IN), (l, 0, 0)),
            _const_spec((None, LRU_K, W_LRU), (l, 0, 0)),
            _const_spec((None, 1, W_LRU), (l, 0, 0)),
            _const_spec((None, None, 2, 2, 256, 256), (l, 0, 0, 0, 0, 0)),
            _const_spec((None, None, 1, W_LRU), (l, 0, 0, 0)),
            _const_spec((None, None, 1, W_LRU), (l, 0, 0, 0)),
            _const_spec((None, None, 1, W_LRU), (l, 0, 0, 0)),
            _const_spec((None, W_POOL, W_POOL), (l, 0, 0)),
            _const_spec((None, 1, W_POOL), (l, 0, 0)),
            _const_spec((None, CONF_K, W_CONF), (l, 0, 0)),
            _const_spec((None, 1, W_CONF), (l, 0, 0)),
            _const_spec((None, 1, W_CONF), (l, 0, 0)),
            _const_spec((None, 1, W_CONF), (l, 0, 0)),
            _const_spec((None, W_CONF, W_CONF), (l, 0, 0)),
            _const_spec((None, D, D), (l, 0, 0)),
            _const_spec((None, 1, D), (l, 0, 0)),
        ]
        s = pl.pallas_call(
            _main_kernel,
            out_shape=jax.ShapeDtypeStruct((NT, NB, D), f32),
            grid=(NCH,),
            in_specs=main_in_specs,
            out_specs=pl.BlockSpec((T, NB, D), lambda i: (i, 0, 0)),
            scratch_shapes=[
                pltpu.VMEM(((T + 2 * H) * NB, D), bf16),
                pltpu.VMEM((T + 2 * H, NB, N_VAL), f32),
                pltpu.VMEM((T, NB, D), f32),
                pltpu.VMEM((T, NB, W_LRU), f32),
                pltpu.VMEM((T + 2 * H, NB, W_CONF), f32),
                pltpu.VMEM((T * NB, W_POOL), bf16),
                pltpu.VMEM((T * NB, W_CONF), bf16),
                pltpu.VMEM((T * NB, D), bf16),
                pltpu.VMEM((NB, W_LRU), f32),
            ],
            compiler_params=cparams,
            name=f"layer_fwd_{l}",
        )(s, s, s, hb, mod, row(g_pre), w_all, lru_conv_w, row(lru_conv_b), wa_t, dir_row(lru_ba),
          dir_row(lru_bx), dir_row(lru_lambda), poolw_t, row(pool_scale), conf_dw, row(conf_dw_b),
          row(conf_ln_g), row(conf_ln_b), pw_b, wout_b, row(g_post))

    return jnp.transpose(s[N_CTX:], (1, 0, 2))
```

```python
import jax
import jax.numpy as jnp
from jax import lax
from jax.experimental import pallas as pl
from jax.experimental.pallas import tpu as pltpu

D = 1024
NB = 8
N_LAT = 2048
N_CTX = 256
DEPTH = 4
GRID_W = 64
W_LRU, W_POOL, W_CONF = 512, 256, 256
N_VAL = W_LRU + W_POOL + 2 * W_CONF
N_IN = N_VAL + D
LRU_K = 4
LRU_C = 8.0
CONF_K = 31
EPS = 1e-6
LN_EPS = 1e-5

TBK = 32
RBK = TBK * NB
HW = 16
HT, HH = 2, 1
TS = 8
NBC = N_CTX // TBK
NBL = N_LAT // TBK
NBK = NBC + NBL
NT = N_CTX + N_LAT
W_WIN = W_LRU + W_POOL + W_CONF
VMEM_LIMIT = 48 * 1024 * 1024

f32 = jnp.float32
bf16 = jnp.bfloat16


def _silu(g):
    return g * jax.nn.sigmoid(g)


def _block_flags(c):
    is_lat = c >= NBC
    first = jnp.logical_or(c == 0, c == NBC)
    last = jnp.logical_or(c == NBC - 1, c == NBK - 1)
    return is_lat, first, last


def _mod_rows(mod_ref, is_lat):
    mod = mod_ref[is_lat.astype(jnp.int32)]
    return mod[:, 0:D], mod[:, D:2 * D], mod[:, 2 * D:3 * D]


def _norm_block(x_ref, gs, sh, hx_ref):
    for sb in range(TBK // TS):
        x = x_ref[sb * TS:(sb + 1) * TS]
        ms = jnp.mean(x * x, axis=-1, keepdims=True)
        hx = (x * lax.rsqrt(ms + EPS)) * gs + sh
        hx_ref[sb * TS * NB:(sb + 1) * TS * NB, :] = hx.reshape(TS * NB, D).astype(bf16)


def _softplus(y):
    return jnp.maximum(y, 0.0) + jnp.log1p(jnp.exp(-jnp.abs(y)))


def _lru_coeffs(xc_ref, a_ref, wa_ref, ba_ref, bx_ref, lam_ref):
    c8 = -LRU_C * _softplus(-lam_ref[...])
    x2 = xc_ref[...].reshape(RBK, W_LRU)
    xb = x2.astype(bf16)
    for kt in range(2):
        sl = slice(256 * kt, 256 * (kt + 1))
        pr = jnp.dot(xb[:, sl], wa_ref[0, kt], preferred_element_type=f32)
        pi = jnp.dot(xb[:, sl], wa_ref[1, kt], preferred_element_type=f32)
        for sb in range(TBK // TS):
            rs = slice(sb * TS * NB, (sb + 1) * TS * NB)
            r = jax.nn.sigmoid(pr[rs] + ba_ref[:, sl])
            ig = jax.nn.sigmoid(pi[rs] + bx_ref[:, sl])
            la = r * c8[:, sl]
            a = jnp.exp(la)
            mult = jnp.sqrt(-jnp.tanh(la) * (1.0 + a * a))
            b = mult * (ig * x2[rs, sl])
            a_ref[sb * TS:(sb + 1) * TS, :, sl] = a.reshape(TS, NB, 256)
            xc_ref[sb * TS:(sb + 1) * TS, :, sl] = b.reshape(TS, NB, 256)


def _rev_block(s):
    return jnp.where(s < NBC, NBC - 1 - s, NBK - 1 - (s - NBC))


def _bwd_in_block(s):
    return _rev_block(jnp.minimum(s, NBK - 1))


def _bwd_out_block(s):
    return _rev_block(jnp.maximum(s - 1, 0))


def _bwd_kernel(xi_ref, mod_ref, gpre_ref, w_ref, lcw_ref, lcb_ref, wa_ref, ba_ref, bx_ref, lam_ref,
                hb_ref, hx_scr, new_scr, win_scr, xc_scr, a_scr, h_scr):
    s = pl.program_id(0)
    lat_i, _, last_i = _block_flags(_bwd_in_block(s))
    _, first_j, _ = _block_flags(_bwd_out_block(s))

    @pl.when(s == 0)
    def _():
        win_scr[...] = jnp.zeros_like(win_scr)
        h_scr[...] = jnp.zeros_like(h_scr)

    sh, sc, _ = _mod_rows(mod_ref, lat_i)
    _norm_block(xi_ref, (gpre_ref[...] * (1.0 + sc))[None], sh[None], hx_scr)
    v = jnp.dot(hx_scr[...], w_ref[...], preferred_element_type=f32).reshape(TBK, NB, W_LRU)
    new_scr[...] = v
    win_scr[0:HT] = jnp.where(first_j, 0.0, v[TBK - HT:TBK])

    for sb in range(TBK // TS):
        acc = jnp.zeros((TS, NB, W_LRU), f32) + lcb_ref[...][None]
        for k in range(LRU_K):
            acc = acc + win_scr[sb * TS + k:sb * TS + k + TS] * lcw_ref[k:k + 1, :][None]
        xc_scr[sb * TS:(sb + 1) * TS] = acc
    _lru_coeffs(xc_scr, a_scr, wa_ref, ba_ref, bx_ref, lam_ref)
    h = jnp.where(s == 1, 0.0, h_scr[...])
    for t in range(TBK - 1, -1, -1):
        h = a_scr[t] * h + xc_scr[t]
        hb_ref[t] = h
    h_scr[...] = h

    win_scr[HT + TBK:HT + TBK + HH] = jnp.where(last_i, 0.0, win_scr[HT:HT + HH])
    win_scr[HT:HT + TBK] = new_scr[...]


def _fwd_kernel(xi_ref, xj_ref, hb_ref, mod_ref, gpre_ref, w_ref, lcw_ref, lcb_ref, wa_ref, ba_ref,
                bx_ref, lam_ref, poolw_ref, pscale_ref, cdw_ref, cdb_ref, lng_ref, lnb_ref, pw_ref,
                wout_ref, gpost_ref, out_ref,
                hx_new, hx_prev, new_scr, win_scr, xc_scr, a_scr, d_scr, c_scr, z_scr, h_scr):
    i = pl.program_id(0)
    bi = jnp.minimum(i, NBK - 1)
    bj = jnp.maximum(i - 1, 0)
    lat_i, first_i, _ = _block_flags(bi)
    lat_j, _, last_j = _block_flags(bj)

    @pl.when(i == 0)
    def _():
        win_scr[...] = jnp.zeros_like(win_scr)
        hx_prev[...] = jnp.zeros_like(hx_prev)
        h_scr[...] = jnp.zeros_like(h_scr)

    sh, sc, _ = _mod_rows(mod_ref, lat_i)
    _norm_block(xi_ref, (gpre_ref[...] * (1.0 + sc))[None], sh[None], hx_new)
    v = jnp.dot(hx_new[...], w_ref[:, 0:N_VAL], preferred_element_type=f32)
    new_scr[:, :, 0:W_LRU + W_POOL] = v[:, 0:W_LRU + W_POOL].reshape(TBK, NB, W_LRU + W_POOL)
    zc = v[:, W_LRU + W_POOL:W_WIN] * jax.nn.sigmoid(v[:, W_WIN:N_VAL])
    new_scr[:, :, W_LRU + W_POOL:W_WIN] = zc.reshape(TBK, NB, W_CONF)
    win_scr[HW + TBK:HW + TBK + HW] = jnp.where(last_j, 0.0, new_scr[0:HW])

    _, _, gt = _mod_rows(mod_ref, lat_j)
    g = jnp.dot(hx_prev[...], w_ref[:, N_VAL:N_IN], preferred_element_type=f32)

    def g3(sb, c0, c1):
        return g[sb * TS * NB:(sb + 1) * TS * NB, c0:c1].reshape(TS, NB, c1 - c0)

    for sb in range(TBK // TS):
        o = HW + sb * TS - LRU_K // 2
        acc = jnp.zeros((TS, NB, W_LRU), f32) + lcb_ref[...][None]
        for k in range(LRU_K):
            acc = acc + win_scr[o + k:o + k + TS, :, 0:W_LRU] * lcw_ref[k:k + 1, :][None]
        xc_scr[sb * TS:(sb + 1) * TS] = acc
    _lru_coeffs(xc_scr, a_scr, wa_ref, ba_ref, bx_ref, lam_ref)
    h = jnp.where(i == 1, 0.0, h_scr[...])
    for t in range(TBK):
        h = a_scr[t] * h + xc_scr[t]
        xc_scr[t] = h
    h_scr[...] = h
    for sb in range(TBK // TS):
        y = xc_scr[sb * TS:(sb + 1) * TS] + hb_ref[sb * TS:(sb + 1) * TS]
        zz = y * _silu(g3(sb, 0, W_LRU))
        z_scr[sb * TS * NB:(sb + 1) * TS * NB, 0:W_LRU] = zz.reshape(TS * NB, W_LRU).astype(bf16)

    nseq = jnp.where(lat_j, N_LAT, N_CTX)
    tseq0 = (bj - jnp.where(lat_j, NBC, 0)) * TBK
    hi_half = lax.broadcasted_iota(jnp.int32, (1, 1, 128), 2) >= 64
    hw_a = jnp.where(hi_half, 2, 1)
    hw_b = jnp.where(hi_half, 8, 4)
    ca, cb = W_LRU, W_LRU + 128
    for sb in range(TBK // TS):
        base = HW + sb * TS
        tg = lax.broadcasted_iota(jnp.int32, (TS, NB, 128), 0) + (tseq0 + sb * TS)

        def count(hw):
            return (jnp.minimum(tg + hw, nseq) - jnp.maximum(tg - hw, 0)).astype(f32)

        def ld(k, c0):
            return win_scr[base + k:base + k + TS, :, c0:c0 + 128]

        va = ld(0, ca)
        sa = (ld(-1, ca) + va) + jnp.where(hi_half, ld(-2, ca) + ld(1, ca), 0.0)
        vb = ld(0, cb)
        inner = vb
        for k in (-4, -3, -2, -1, 1, 2, 3):
            inner = inner + ld(k, cb)
        outer = ld(-8, cb)
        for k in (-7, -6, -5, 4, 5, 6, 7):
            outer = outer + ld(k, cb)
        sbv = inner + jnp.where(hi_half, outer, 0.0)
        da = sa / count(hw_a) - va
        db = sbv / count(hw_b) - vb
        rs = slice(sb * TS * NB, (sb + 1) * TS * NB)
        d_scr[rs, 0:128] = da.reshape(TS * NB, 128).astype(bf16)
        d_scr[rs, 128:256] = db.reshape(TS * NB, 128).astype(bf16)

    cz = W_LRU + W_POOL
    for sb in range(TBK // TS):
        o = HW + sb * TS - CONF_K // 2
        acc = jnp.zeros((TS, NB, W_CONF), f32) + cdb_ref[...][None]
        for k in range(CONF_K):
            acc = acc + win_scr[o + k:o + k + TS, :, cz:cz + W_CONF] * cdw_ref[k:k + 1, :][None]
        mu = jnp.mean(acc, axis=-1, keepdims=True)
        dz = acc - mu
        var = jnp.mean(dz * dz, axis=-1, keepdims=True)
        zn = dz * lax.rsqrt(var + LN_EPS) * lng_ref[...][None] + lnb_ref[...][None]
        c_scr[sb * TS * NB:(sb + 1) * TS * NB, :] = _silu(zn).reshape(TS * NB, W_CONF).astype(bf16)

    yp = jnp.dot(d_scr[...], poolw_ref[...], preferred_element_type=f32) * pscale_ref[...]
    yc = jnp.dot(c_scr[...], pw_ref[...], preferred_element_type=f32)
    z_scr[:, W_LRU:W_LRU + W_POOL] = (yp * _silu(g[:, W_LRU:W_LRU + W_POOL])).astype(bf16)
    z_scr[:, W_LRU + W_POOL:D] = (yc * _silu(g[:, W_LRU + W_POOL:D])).astype(bf16)

    ov = jnp.dot(z_scr[...], wout_ref[...], preferred_element_type=f32)
    for sb in range(TBK // TS):
        ob = ov[sb * TS * NB:(sb + 1) * TS * NB]
        ms = jnp.mean(ob * ob, axis=-1, keepdims=True)
        y = (ob * lax.rsqrt(ms + EPS)) * gpost_ref[...]
        out_ref[sb * TS:(sb + 1) * TS] = xj_ref[sb * TS:(sb + 1) * TS] + gt[None] * y.reshape(TS, NB, D)

    win_scr[0:HW] = jnp.where(first_i, 0.0, win_scr[TBK:TBK + HW])
    win_scr[HW:HW + TBK] = new_scr[...]
    hx_prev[...] = hx_new[...]


def _mod_kernel(cond_ref, w_ref, b_ref, out_ref):
    s = _silu(cond_ref[...])
    out_ref[...] = jnp.dot(s, w_ref[...], preferred_element_type=f32,
                           precision=lax.Precision.HIGHEST) + b_ref[...]


def _pos_kernel(x_ref, pos_ref, out_ref):
    out_ref[...] = x_ref[...] + pos_ref[...]


def _pos_embed(n):
    rows = n // GRID_W
    row = jnp.repeat(jnp.arange(rows, dtype=f32), GRID_W)
    col = jnp.tile(jnp.arange(GRID_W, dtype=f32), rows)
    q = D // 4
    omega = 1.0 / (10000.0 ** (jnp.arange(q, dtype=f32) / q))

    def enc(p):
        ang = p[:, None] * omega[None, :]
        return jnp.concatenate([jnp.sin(ang), jnp.cos(ang)], axis=-1)
    return jnp.concatenate([enc(row), enc(col)], axis=-1)


def _block_diag(w, per_tile):
    *lead, g, n, _ = w.shape
    w = w.reshape(*lead, g // per_tile, per_tile, n, n)
    eye = jnp.eye(per_tile, dtype=w.dtype)
    out = jnp.einsum('...jio,jm->...jimo', w, eye)
    return out.reshape(*lead, g // per_tile, per_tile * n, per_tile * n)


def _const_spec(block, index):
    return pl.BlockSpec(block, lambda i: index, pipeline_mode=pl.Buffered(1))


def kernel(x, c, ctx, c_ctx, w_mod, b_mod, g_pre, w_in, lru_conv_w, lru_conv_b, lru_wa, lru_ba, lru_wx,
           lru_bx, lru_lambda, pool_w, pool_scale, conf_dw, conf_dw_b, conf_ln_g, conf_ln_b, conf_pw,
           w_out, g_post):
    L = DEPTH
    cols = [(0, 512), (1024, 1280), (1536, 2048), (512, 1024), (1280, 1536), (2048, 2304)]
    w_all = jnp.concatenate([w_in[:, :, a:b] for a, b in cols], axis=-1).astype(bf16)
    wa_t = jnp.stack([_block_diag(lru_wa, 4), _block_diag(lru_wx, 4)], axis=2).astype(bf16)
    poolw_t = _block_diag(pool_w, 4)[:, 0].astype(bf16)
    pw_b = conf_pw.astype(bf16)
    wout_b = w_out.astype(bf16)
    row = lambda a: a[:, None, :]
    dir_row = lambda a: a[:, :, None, :]

    cond = jnp.concatenate([jnp.broadcast_to(c_ctx[None, :], (NB, D)), c], axis=0)
    tn = 1024
    mod = pl.pallas_call(
        _mod_kernel,
        out_shape=jax.ShapeDtypeStruct((L, 2 * NB, 3 * D), f32),
        grid=(L, 3 * D // tn),
        in_specs=[pl.BlockSpec((2 * NB, D), lambda l, n: (0, 0)),
                  pl.BlockSpec((None, D, tn), lambda l, n: (l, 0, n)),
                  pl.BlockSpec((None, 1, tn), lambda l, n: (l, 0, n))],
        out_specs=pl.BlockSpec((None, 2 * NB, tn), lambda l, n: (l, 0, n)),
        name="modulation",
    )(cond, w_mod, row(b_mod))
    mod = mod.reshape(L, 2, NB, 3 * D)

    tp = 512
    x0 = pl.pallas_call(
        _pos_kernel,
        out_shape=jax.ShapeDtypeStruct(x.shape, f32),
        grid=(NB, N_LAT // tp),
        in_specs=[pl.BlockSpec((None, tp, D), lambda b, n: (b, n, 0)),
                  pl.BlockSpec((tp, D), lambda b, n: (n, 0))],
        out_specs=pl.BlockSpec((None, tp, D), lambda b, n: (b, n, 0)),
        name="pos_add",
    )(x, _pos_embed(N_LAT))
    s = jnp.concatenate([jnp.transpose(ctx, (1, 0, 2)), jnp.transpose(x0, (1, 0, 2))], axis=0)

    cparams = pltpu.CompilerParams(dimension_semantics=("arbitrary",), vmem_limit_bytes=VMEM_LIMIT)

    for l in range(L):
        def lru_specs(d):
            return [
                _const_spec((None, LRU_K, W_LRU), (l, 0, 0)),
                _const_spec((None, 1, W_LRU), (l, 0, 0)),
                _const_spec((None, None, 2, 2, 256, 256), (l, d, 0, 0, 0, 0)),
                _const_spec((None, None, 1, W_LRU), (l, d, 0, 0)),
                _const_spec((None, None, 1, W_LRU), (l, d, 0, 0)),
                _const_spec((None, None, 1, W_LRU), (l, d, 0, 0)),
            ]
        lru_args = (lru_conv_w, row(lru_conv_b), wa_t, dir_row(lru_ba), dir_row(lru_bx), dir_row(lru_lambda))
        mod_specs = [_const_spec((None, 2, NB, 3 * D), (l, 0, 0, 0)), _const_spec((None, 1, D), (l, 0, 0))]

        hb = pl.pallas_call(
            _bwd_kernel,
            out_shape=jax.ShapeDtypeStruct((NT, NB, W_LRU), f32),
            grid=(NBK + 1,),
            in_specs=[pl.BlockSpec((TBK, NB, D), lambda s_: (_bwd_in_block(s_), 0, 0)),
                      *mod_specs,
                      _const_spec((None, D, W_LRU), (l, 0, 0)),
                      *lru_specs(1)],
            out_specs=pl.BlockSpec((TBK, NB, W_LRU), lambda s_: (_bwd_out_block(s_), 0, 0)),
            scratch_shapes=[
                pltpu.VMEM((RBK, D), bf16),
                pltpu.VMEM((TBK, NB, W_LRU), f32),
                pltpu.VMEM((HT + TBK + HH, NB, W_LRU), f32),
                pltpu.VMEM((TBK, NB, W_LRU), f32),
                pltpu.VMEM((TBK, NB, W_LRU), f32),
                pltpu.VMEM((NB, W_LRU), f32),
            ],
            compiler_params=cparams,
            name=f"lru_bwd_{l}",
        )(s, mod, row(g_pre), w_all, *lru_args)

        s = pl.pallas_call(
            _fwd_kernel,
            out_shape=jax.ShapeDtypeStruct((NT, NB, D), f32),
            grid=(NBK + 1,),
            in_specs=[pl.BlockSpec((TBK, NB, D), lambda i: (jnp.minimum(i, NBK - 1), 0, 0)),
                      pl.BlockSpec((TBK, NB, D), lambda i: (jnp.maximum(i - 1, 0), 0, 0)),
                      pl.BlockSpec((TBK, NB, W_LRU), lambda i: (jnp.maximum(i - 1, 0), 0, 0)),
                      *mod_specs,
                      _const_spec((None, D, N_IN), (l, 0, 0)),
                      *lru_specs(0),
                      _const_spec((None, W_POOL, W_POOL), (l, 0, 0)),
                      _const_spec((None, 1, W_POOL), (l, 0, 0)),
                      _const_spec((None, CONF_K, W_CONF), (l, 0, 0)),
                      _const_spec((None, 1, W_CONF), (l, 0, 0)),
                      _const_spec((None, 1, W_CONF), (l, 0, 0)),
                      _const_spec((None, 1, W_CONF), (l, 0, 0)),
                      _const_spec((None, W_CONF, W_CONF), (l, 0, 0)),
                      _const_spec((None, D, D), (l, 0, 0)),
                      _const_spec((None, 1, D), (l, 0, 0))],
            out_specs=pl.BlockSpec((TBK, NB, D), lambda i: (jnp.maximum(i - 1, 0), 0, 0)),
            scratch_shapes=[
                pltpu.VMEM((RBK, D), bf16),
                pltpu.VMEM((RBK, D), bf16),
                pltpu.VMEM((TBK, NB, W_WIN), f32),
                pltpu.VMEM((HW + TBK + HW, NB, W_WIN), f32),
                pltpu.VMEM((TBK, NB, W_LRU), f32),
                pltpu.VMEM((TBK, NB, W_LRU), f32),
                pltpu.VMEM((RBK, W_POOL), bf16),
                pltpu.VMEM((RBK, W_CONF), bf16),
                pltpu.VMEM((RBK, D), bf16),
                pltpu.VMEM((NB, W_LRU), f32),
            ],
            compiler_params=cparams,
            name=f"layer_fwd_{l}",
        )(s, s, hb, mod, row(g_pre), w_all, *lru_args, poolw_t, row(pool_scale), conf_dw, row(conf_dw_b),
          row(conf_ln_g), row(conf_ln_b), pw_b, wout_b, row(g_post))

    return jnp.transpose(s[N_CTX:], (1, 0, 2))
```

```python
import jax
import jax.numpy as jnp
from jax import lax
from jax.experimental import pallas as pl
from jax.experimental.pallas import tpu as pltpu

D = 1024
NB = 8
N_LAT = 2048
N_CTX = 256
DEPTH = 4
GRID_W = 64
W_LRU, W_POOL, W_CONF = 512, 256, 256
N_VAL = W_LRU + W_POOL + 2 * W_CONF
N_IN = N_VAL + D
LRU_K = 4
LRU_C = 8.0
CONF_K = 31
EPS = 1e-6
LN_EPS = 1e-5

TBK = 32
RBK = TBK * NB
HW = 16
HT, HH = 2, 1
TS = 8
NSB = TBK // TS
NBC = N_CTX // TBK
NBL = N_LAT // TBK
NBK = NBC + NBL
NT = N_CTX + N_LAT
W_WIN = W_LRU + W_POOL + W_CONF
MIX_LAG = 2
OUT_LAG = 3
VMEM_LIMIT = 48 * 1024 * 1024

f32 = jnp.float32
bf16 = jnp.bfloat16


def _silu(g):
    return g * jax.nn.sigmoid(g)


def _block_flags(c):
    is_lat = c >= NBC
    first = jnp.logical_or(c == 0, c == NBC)
    last = jnp.logical_or(c == NBC - 1, c == NBK - 1)
    return is_lat, first, last


def _clamp_block(c):
    return jnp.clip(c, 0, NBK - 1)


def _mod_rows(mod_ref, is_lat):
    mod = mod_ref[is_lat.astype(jnp.int32)]
    return mod[:, 0:D], mod[:, D:2 * D], mod[:, 2 * D:3 * D]


def _rows(sb):
    if isinstance(sb, int):
        return slice(sb * TS * NB, (sb + 1) * TS * NB)
    return pl.ds(pl.multiple_of(sb * (TS * NB), TS * NB), TS * NB)


def _steps(sb):
    if isinstance(sb, int):
        return slice(sb * TS, (sb + 1) * TS)
    return pl.ds(pl.multiple_of(sb * TS, TS), TS)


def _tile(n):
    return slice(256 * n, 256 * (n + 1))


def _norm_piece(x_ref, gs, sh, hx_ref, sb):
    x = x_ref[_steps(sb)]
    ms = jnp.mean(x * x, axis=-1, keepdims=True)
    hx = (x * lax.rsqrt(ms + EPS)) * gs + sh
    hx_ref[_rows(sb), :] = hx.reshape(TS * NB, D).astype(bf16)


def _softplus(y):
    return jnp.maximum(y, 0.0) + jnp.log1p(jnp.exp(-jnp.abs(y)))


def _dwconv_piece(win_ref, o, cols, taps_ref, bias_ref, n_taps):
    acc = [None] * TS
    for tau in range(TS + n_taps - 1):
        zt = win_ref[o + tau, :, cols]
        for t in range(max(0, tau - n_taps + 1), min(TS, tau + 1)):
            term = zt * taps_ref[tau - t]
            acc[t] = term if acc[t] is None else acc[t] + term
    return jnp.stack(acc) + bias_ref[...][None]


def _conv4_piece(win_ref, off, lcw_ref, lcb_ref, xc_ref, sb):
    o = off + sb * TS - LRU_K // 2
    xc_ref[_steps(sb)] = _dwconv_piece(win_ref, o, slice(0, W_LRU), lcw_ref, lcb_ref, LRU_K)


def _coef_dots(xc_ref, wa_ref, kt, p_ref):
    xb = xc_ref[:, :, _tile(kt)].reshape(RBK, 256).astype(bf16)
    p_ref[:, _tile(kt)] = jnp.dot(xb, wa_ref[0, kt], preferred_element_type=f32)
    p_ref[:, _tile(2 + kt)] = jnp.dot(xb, wa_ref[1, kt], preferred_element_type=f32)


def _coef_piece(p_ref, ba_ref, bx_ref, c8, kt, sb, a_ref, xc_ref):
    x2 = xc_ref[_steps(sb), :, _tile(kt)].reshape(TS * NB, 256)
    r = jax.nn.sigmoid(p_ref[_rows(sb), _tile(kt)] + ba_ref[:, _tile(kt)])
    ig = jax.nn.sigmoid(p_ref[_rows(sb), _tile(2 + kt)] + bx_ref[:, _tile(kt)])
    la = r * c8[:, _tile(kt)]
    a = jnp.exp(la)
    m2 = -jnp.tanh(la) * (1.0 + a * a)
    mult = jnp.where(m2 > 0.0, m2 * lax.rsqrt(m2), 0.0)
    b = mult * (ig * x2)
    a_ref[_steps(sb), :, _tile(kt)] = a.reshape(TS, NB, 256)
    xc_ref[_steps(sb), :, _tile(kt)] = b.reshape(TS, NB, 256)


def _rev_block(s):
    s = jnp.clip(s, 0, NBK - 1)
    return jnp.where(s < NBC, NBC - 1 - s, NBK - 1 - (s - NBC))


def _bwd_kernel(xi_ref, mod_ref, gpre_ref, w_ref, lcw_ref, lcb_ref, wa_ref, ba_ref, bx_ref, lam_ref,
                hb_ref, hx_scr, new_scr, mid_scr, win_scr, xc_scr, a_scr, p_scr, h_scr):
    s = pl.program_id(0)
    lat_i, _, _ = _block_flags(_rev_block(s))
    _, first_m, last_m = _block_flags(_rev_block(s - 1))

    @pl.when(s == 0)
    def _():
        win_scr[...] = jnp.zeros_like(win_scr)
        mid_scr[...] = jnp.zeros_like(mid_scr)
        h_scr[...] = jnp.zeros_like(h_scr)

    sh, sc, _ = _mod_rows(mod_ref, lat_i)
    gs = (gpre_ref[...] * (1.0 + sc))[None]
    sh = sh[None]
    c8 = -LRU_C * _softplus(-lam_ref[...])

    for sb in range(NSB):
        _conv4_piece(win_scr, HT, lcw_ref, lcb_ref, xc_scr, sb)
    for kt in range(2):
        _coef_dots(xc_scr, wa_ref, kt, p_scr)
    for sb in range(NSB):
        _norm_piece(xi_ref, gs, sh, hx_scr, sb)
        _coef_piece(p_scr, ba_ref, bx_ref, c8, 0, sb, a_scr, xc_scr)
    for n in range(2):
        v = jnp.dot(hx_scr[...], w_ref[:, _tile(n)], preferred_element_type=f32)
        new_scr[:, :, _tile(n)] = v.reshape(TBK, NB, 256)
        for sb in (2 * n, 2 * n + 1):
            _coef_piece(p_scr, ba_ref, bx_ref, c8, 1, sb, a_scr, xc_scr)
    h = jnp.where(s == 2, 0.0, h_scr[...])
    for t in range(TBK - 1, -1, -1):
        h = a_scr[t] * h + xc_scr[t]
        hb_ref[t] = h
    h_scr[...] = h

    win_scr[HT + TBK:HT + TBK + HH] = jnp.where(last_m, 0.0, win_scr[HT:HT + HH])
    win_scr[HT:HT + TBK] = mid_scr[...]
    win_scr[0:HT] = jnp.where(first_m, 0.0, new_scr[TBK - HT:TBK])
    mid_scr[...] = new_scr[...]


def _fwd_kernel(xi_ref, xo_ref, hb_ref, mod_ref, gpre_ref, w_ref, lcw_ref, lcb_ref, wa_ref, ba_ref,
                bx_ref, lam_ref, poolw_ref, pscale_ref, cdw_ref, cdb_ref, lng_ref, lnb_ref, pw_ref,
                wout_ref, gpost_ref, out_ref,
                hx_new, hx_mid, hx_mix, new_scr, mid_scr, win_scr, xc_scr, a_scr, p_scr, g_scr, ov_scr,
                d_scr, c_scr, z_new, z_prev, h_scr):
    i = pl.program_id(0)
    lat_i, _, _ = _block_flags(_clamp_block(i))
    _, first_m, last_m = _block_flags(_clamp_block(i - 1))
    bj = _clamp_block(i - MIX_LAG)
    lat_j, _, _ = _block_flags(bj)
    lat_o, _, _ = _block_flags(_clamp_block(i - OUT_LAG))

    @pl.when(i == 0)
    def _():
        win_scr[...] = jnp.zeros_like(win_scr)
        mid_scr[...] = jnp.zeros_like(mid_scr)
        hx_mid[...] = jnp.zeros_like(hx_mid)
        hx_mix[...] = jnp.zeros_like(hx_mix)
        z_prev[...] = jnp.zeros_like(z_prev)
        h_scr[...] = jnp.zeros_like(h_scr)

    sh, sc, _ = _mod_rows(mod_ref, lat_i)
    gs = (gpre_ref[...] * (1.0 + sc))[None]
    sh = sh[None]
    _, _, gt = _mod_rows(mod_ref, lat_o)
    gg = gt * gpost_ref[...]
    c8 = -LRU_C * _softplus(-lam_ref[...])

    def conf_piece(sb):
        o = HW + sb * TS - CONF_K // 2
        acc = _dwconv_piece(win_scr, o, slice(W_LRU + W_POOL, W_WIN), cdw_ref, cdb_ref, CONF_K)
        mu = jnp.mean(acc, axis=-1, keepdims=True)
        dz = acc - mu
        var = jnp.mean(dz * dz, axis=-1, keepdims=True)
        zn = dz * lax.rsqrt(var + LN_EPS) * lng_ref[...][None] + lnb_ref[...][None]
        c_scr[_rows(sb), :] = _silu(zn).reshape(TS * NB, W_CONF).astype(bf16)

    nseq = jnp.where(lat_j, N_LAT, N_CTX)
    tseq0 = (bj - jnp.where(lat_j, NBC, 0)) * TBK
    hi_half = lax.broadcasted_iota(jnp.int32, (1, 1, 128), 2) >= 64
    hw_a = jnp.where(hi_half, 2, 1)
    hw_b = jnp.where(hi_half, 8, 4)

    def pool_piece(sb):
        ca, cb = W_LRU, W_LRU + 128
        base = HW + sb * TS
        tg = lax.broadcasted_iota(jnp.int32, (TS, NB, 128), 0) + (tseq0 + sb * TS)

        def count(hw):
            return (jnp.minimum(tg + hw, nseq) - jnp.maximum(tg - hw, 0)).astype(f32)

        def ld(k, c0):
            return win_scr[base + k:base + k + TS, :, c0:c0 + 128]

        va = ld(0, ca)
        sa = (ld(-1, ca) + va) + jnp.where(hi_half, ld(-2, ca) + ld(1, ca), 0.0)
        vb = ld(0, cb)
        inner = vb
        for k in (-4, -3, -2, -1, 1, 2, 3):
            inner = inner + ld(k, cb)
        outer = ld(-8, cb)
        for k in (-7, -6, -5, 4, 5, 6, 7):
            outer = outer + ld(k, cb)
        sbv = inner + jnp.where(hi_half, outer, 0.0)
        da = sa / count(hw_a) - va
        db = sbv / count(hw_b) - vb
        d_scr[_rows(sb), 0:128] = da.reshape(TS * NB, 128).astype(bf16)
        d_scr[_rows(sb), 128:256] = db.reshape(TS * NB, 128).astype(bf16)

    for n in range(4):
        ov_scr[n] = jnp.dot(z_prev[...], wout_ref[n], preferred_element_type=f32)
        conf_piece(n)

    for n in range(4):
        g_scr[:, _tile(n)] = jnp.dot(hx_mix[...], w_ref[:, N_VAL + 256 * n:N_VAL + 256 * (n + 1)],
                                     preferred_element_type=f32)
        _norm_piece(xi_ref, gs, sh, hx_new, n)
        _conv4_piece(win_scr, HW, lcw_ref, lcb_ref, xc_scr, n)

    for kt in range(2):
        _coef_dots(xc_scr, wa_ref, kt, p_scr)

    def value_tile(n):
        return jnp.dot(hx_new[...], w_ref[:, _tile(n)], preferred_element_type=f32)

    def post_piece(sb):
        obs = [ov_scr[n, _rows(sb), :] for n in range(4)]
        ssq = obs[0] * obs[0]
        for ob in obs[1:]:
            ssq = ssq + ob * ob
        rs = lax.rsqrt(jnp.sum(ssq, axis=-1, keepdims=True) * (1.0 / D) + EPS)
        for n, ob in enumerate(obs):
            y = (ob * rs).reshape(TS, NB, 256)
            out_ref[_steps(sb), :, _tile(n)] = xo_ref[_steps(sb), :, _tile(n)] + gg[:, _tile(n)][None] * y

    for n in range(3):
        new_scr[:, :, _tile(n)] = value_tile(n).reshape(TBK, NB, 256)
        post_piece(n)
        for sb in (2 * n, 2 * n + 1):
            _coef_piece(p_scr, ba_ref, bx_ref, c8, sb // NSB, sb % NSB, a_scr, xc_scr)
    vc = value_tile(3)
    post_piece(3)
    for sb in (2, 3):
        _coef_piece(p_scr, ba_ref, bx_ref, c8, 1, sb, a_scr, xc_scr)
    zc = vc * jax.nn.sigmoid(value_tile(4))
    new_scr[:, :, _tile(3)] = zc.reshape(TBK, NB, W_CONF)

    h = jnp.where(i == MIX_LAG, 0.0, h_scr[...])
    for sb in range(NSB):
        for t in range(sb * TS, (sb + 1) * TS):
            h = a_scr[t] * h + xc_scr[t]
            xc_scr[t] = h
        pool_piece(sb)
        y = xc_scr[_steps(sb)] + hb_ref[_steps(sb)]
        for n in range(2):
            zz = y[:, :, _tile(n)].reshape(TS * NB, 256) * _silu(g_scr[_rows(sb), _tile(n)])
            z_new[_rows(sb), _tile(n)] = zz.astype(bf16)
    h_scr[...] = h

    yp = jnp.dot(d_scr[...], poolw_ref[...], preferred_element_type=f32) * pscale_ref[...]
    yc = jnp.dot(c_scr[...], pw_ref[...], preferred_element_type=f32)
    z_new[:, _tile(2)] = (yp * _silu(g_scr[:, _tile(2)])).astype(bf16)
    z_new[:, _tile(3)] = (yc * _silu(g_scr[:, _tile(3)])).astype(bf16)

    win_scr[0:HW] = jnp.where(first_m, 0.0, win_scr[TBK:TBK + HW])
    win_scr[HW:HW + TBK] = mid_scr[...]
    win_scr[HW + TBK:HW + TBK + HW] = jnp.where(last_m, 0.0, new_scr[0:HW])
    mid_scr[...] = new_scr[...]
    hx_mix[...] = hx_mid[...]
    hx_mid[...] = hx_new[...]
    z_prev[...] = z_new[...]


def _mod_kernel(cond_ref, w_ref, b_ref, out_ref):
    s = _silu(cond_ref[...])
    out_ref[...] = jnp.dot(s, w_ref[...], preferred_element_type=f32,
                           precision=lax.Precision.HIGHEST) + b_ref[...]


TP = 256
RCH = 64


def _ingest_kernel(ctx_ref, x_ref, pos_ref, out_ref):
    i = pl.program_id(0)

    @pl.when(i == 0)
    def _():
        for b in range(NB):
            for r in range(0, TP, RCH):
                out_ref[r:r + RCH, b, :] = ctx_ref[b, r:r + RCH, :]

    @pl.when(i > 0)
    def _():
        for b in range(NB):
            for r in range(0, TP, RCH):
                out_ref[r:r + RCH, b, :] = x_ref[b, r:r + RCH, :] + pos_ref[r:r + RCH, :]


def _egress_kernel(s_ref, out_ref):
    for b in range(NB):
        for r in range(0, TP, RCH):
            out_ref[b, r:r + RCH, :] = s_ref[r:r + RCH, b, :]


def _pos_embed(n):
    rows = n // GRID_W
    q = D // 4
    omega = 1.0 / (10000.0 ** (jnp.arange(q, dtype=f32) / q))

    def enc(p):
        ang = p[:, None] * omega[None, :]
        return jnp.concatenate([jnp.sin(ang), jnp.cos(ang)], axis=-1)
    enc_row = jnp.repeat(enc(jnp.arange(rows, dtype=f32)), GRID_W, axis=0)
    enc_col = jnp.tile(enc(jnp.arange(GRID_W, dtype=f32)), (rows, 1))
    return jnp.concatenate([enc_row, enc_col], axis=-1)


def _block_diag(w, per_tile):
    *lead, g, n, _ = w.shape
    w = w.reshape(*lead, g // per_tile, per_tile, n, n)
    eye = jnp.eye(per_tile, dtype=w.dtype)
    out = jnp.einsum('...jio,jm->...jimo', w, eye)
    return out.reshape(*lead, g // per_tile, per_tile * n, per_tile * n)


def _const_spec(block, index):
    return pl.BlockSpec(block, lambda i: index, pipeline_mode=pl.Buffered(1))


def kernel(x, c, ctx, c_ctx, w_mod, b_mod, g_pre, w_in, lru_conv_w, lru_conv_b, lru_wa, lru_ba, lru_wx,
           lru_bx, lru_lambda, pool_w, pool_scale, conf_dw, conf_dw_b, conf_ln_g, conf_ln_b, conf_pw,
           w_out, g_post):
    L = DEPTH
    cols = [(0, 512), (1024, 1280), (1536, 2048), (512, 1024), (1280, 1536), (2048, 2304)]
    w_all = jnp.concatenate([w_in[:, :, a:b] for a, b in cols], axis=-1).astype(bf16)
    wa_t = jnp.stack([_block_diag(lru_wa, 4), _block_diag(lru_wx, 4)], axis=2).astype(bf16)
    poolw_t = _block_diag(pool_w, 4)[:, 0].astype(bf16)
    pw_b = conf_pw.astype(bf16)
    wout_b = w_out.astype(bf16).reshape(L, D, 4, 256).transpose(0, 2, 1, 3)
    row = lambda a: a[:, None, :]
    dir_row = lambda a: a[:, :, None, :]
    taps = lambda a: jnp.broadcast_to(a[:, :, None, :], (*a.shape[:2], NB, a.shape[2]))
    lcw_t = taps(lru_conv_w)
    cdw_t = taps(conf_dw)

    cond = jnp.concatenate([jnp.broadcast_to(c_ctx[None, :], (NB, D)), c], axis=0)
    tn = 1024
    mod = pl.pallas_call(
        _mod_kernel,
        out_shape=jax.ShapeDtypeStruct((L, 2 * NB, 3 * D), f32),
        grid=(L, 3 * D // tn),
        in_specs=[pl.BlockSpec((2 * NB, D), lambda l, n: (0, 0)),
                  pl.BlockSpec((None, D, tn), lambda l, n: (l, 0, n)),
                  pl.BlockSpec((None, 1, tn), lambda l, n: (l, 0, n))],
        out_specs=pl.BlockSpec((None, 2 * NB, tn), lambda l, n: (l, 0, n)),
        name="modulation",
    )(cond, w_mod, row(b_mod))
    mod = mod.reshape(L, 2, NB, 3 * D)

    assert N_CTX == TP
    s = pl.pallas_call(
        _ingest_kernel,
        out_shape=jax.ShapeDtypeStruct((NT, NB, D), f32),
        grid=(NT // TP,),
        in_specs=[pl.BlockSpec((NB, TP, D), lambda i: (0, 0, 0)),
                  pl.BlockSpec((NB, TP, D), lambda i: (0, jnp.maximum(i - 1, 0), 0)),
                  pl.BlockSpec((TP, D), lambda i: (jnp.maximum(i - 1, 0), 0))],
        out_specs=pl.BlockSpec((TP, NB, D), lambda i: (i, 0, 0)),
        compiler_params=pltpu.CompilerParams(dimension_semantics=("arbitrary",), vmem_limit_bytes=VMEM_LIMIT),
        name="ingest",
    )(ctx, x, _pos_embed(N_LAT))

    cparams = pltpu.CompilerParams(dimension_semantics=("arbitrary",), vmem_limit_bytes=VMEM_LIMIT)

    for l in range(L):
        def lru_specs(d):
            return [
                _const_spec((None, LRU_K, NB, W_LRU), (l, 0, 0, 0)),
                _const_spec((None, 1, W_LRU), (l, 0, 0)),
                _const_spec((None, None, 2, 2, 256, 256), (l, d, 0, 0, 0, 0)),
                _const_spec((None, None, 1, W_LRU), (l, d, 0, 0)),
                _const_spec((None, None, 1, W_LRU), (l, d, 0, 0)),
                _const_spec((None, None, 1, W_LRU), (l, d, 0, 0)),
            ]
        lru_args = (lcw_t, row(lru_conv_b), wa_t, dir_row(lru_ba), dir_row(lru_bx), dir_row(lru_lambda))
        mod_specs = [_const_spec((None, 2, NB, 3 * D), (l, 0, 0, 0)), _const_spec((None, 1, D), (l, 0, 0))]

        hb = pl.pallas_call(
            _bwd_kernel,
            out_shape=jax.ShapeDtypeStruct((NT, NB, W_LRU), f32),
            grid=(NBK + 2,),
            in_specs=[pl.BlockSpec((TBK, NB, D), lambda s_: (_rev_block(s_), 0, 0)),
                      *mod_specs,
                      _const_spec((None, D, W_LRU), (l, 0, 0)),
                      *lru_specs(1)],
            out_specs=pl.BlockSpec((TBK, NB, W_LRU), lambda s_: (_rev_block(s_ - 2), 0, 0)),
            scratch_shapes=[
                pltpu.VMEM((RBK, D), bf16),
                pltpu.VMEM((TBK, NB, W_LRU), f32),
                pltpu.VMEM((TBK, NB, W_LRU), f32),
                pltpu.VMEM((HT + TBK + HH, NB, W_LRU), f32),
                pltpu.VMEM((TBK, NB, W_LRU), f32),
                pltpu.VMEM((TBK, NB, W_LRU), f32),
                pltpu.VMEM((RBK, 2 * W_LRU), f32),
                pltpu.VMEM((NB, W_LRU), f32),
            ],
            compiler_params=cparams,
            name=f"lru_bwd_{l}",
        )(s, mod, row(g_pre), w_all, *lru_args)

        s = pl.pallas_call(
            _fwd_kernel,
            out_shape=jax.ShapeDtypeStruct((NT, NB, D), f32),
            grid=(NBK + OUT_LAG,),
            in_specs=[pl.BlockSpec((TBK, NB, D), lambda i: (_clamp_block(i), 0, 0)),
                      pl.BlockSpec((TBK, NB, D), lambda i: (_clamp_block(i - OUT_LAG), 0, 0)),
                      pl.BlockSpec((TBK, NB, W_LRU), lambda i: (_clamp_block(i - MIX_LAG), 0, 0)),
                      *mod_specs,
                      _const_spec((None, D, N_IN), (l, 0, 0)),
                      *lru_specs(0),
                      _const_spec((None, W_POOL, W_POOL), (l, 0, 0)),
                      _const_spec((None, 1, W_POOL), (l, 0, 0)),
                      _const_spec((None, CONF_K, NB, W_CONF), (l, 0, 0, 0)),
                      _const_spec((None, 1, W_CONF), (l, 0, 0)),
                      _const_spec((None, 1, W_CONF), (l, 0, 0)),
                      _const_spec((None, 1, W_CONF), (l, 0, 0)),
                      _const_spec((None, W_CONF, W_CONF), (l, 0, 0)),
                      _const_spec((None, 4, D, 256), (l, 0, 0, 0)),
                      _const_spec((None, 1, D), (l, 0, 0))],
            out_specs=pl.BlockSpec((TBK, NB, D), lambda i: (_clamp_block(i - OUT_LAG), 0, 0)),
            scratch_shapes=[
                pltpu.VMEM((RBK, D), bf16),
                pltpu.VMEM((RBK, D), bf16),
                pltpu.VMEM((RBK, D), bf16),
                pltpu.VMEM((TBK, NB, W_WIN), f32),
                pltpu.VMEM((TBK, NB, W_WIN), f32),
                pltpu.VMEM((HW + TBK + HW, NB, W_WIN), f32),
                pltpu.VMEM((TBK, NB, W_LRU), f32),
                pltpu.VMEM((TBK, NB, W_LRU), f32),
                pltpu.VMEM((RBK, 2 * W_LRU), f32),
                pltpu.VMEM((RBK, D), f32),
                pltpu.VMEM((4, RBK, 256), f32),
                pltpu.VMEM((RBK, W_POOL), bf16),
                pltpu.VMEM((RBK, W_CONF), bf16),
                pltpu.VMEM((RBK, D), bf16),
                pltpu.VMEM((RBK, D), bf16),
                pltpu.VMEM((NB, W_LRU), f32),
            ],
            compiler_params=cparams,
            name=f"layer_fwd_{l}",
        )(s, s, hb, mod, row(g_pre), w_all, *lru_args, poolw_t, row(pool_scale), cdw_t, row(conf_dw_b),
          row(conf_ln_g), row(conf_ln_b), pw_b, wout_b, row(g_post))

    return pl.pallas_call(
        _egress_kernel,
        out_shape=jax.ShapeDtypeStruct((NB, N_LAT, D), f32),
        grid=(N_LAT // TP,),
        in_specs=[pl.BlockSpec((TP, NB, D), lambda i: (i + N_CTX // TP, 0, 0))],
        out_specs=pl.BlockSpec((NB, TP, D), lambda i: (0, i, 0)),
        compiler_params=pltpu.CompilerParams(dimension_semantics=("arbitrary",), vmem_limit_bytes=VMEM_LIMIT),
        name="egress",
    )(s)
```

```python
import jax
import jax.numpy as jnp
from jax import lax
from jax.experimental import pallas as pl
from jax.experimental.pallas import tpu as pltpu

D = 1024
NB = 8
N_LAT = 2048
N_CTX = 256
DEPTH = 4
GRID_W = 64
W_LRU, W_POOL, W_CONF = 512, 256, 256
N_VAL = W_LRU + W_POOL + 2 * W_CONF
N_IN = N_VAL + D
LRU_K = 4
LRU_C = 8.0
CONF_K = 31
EPS = 1e-6
LN_EPS = 1e-5

TBK = 64
RBK = TBK * NB
HW = 16
HT, HH = 2, 1
TS = 8
NSB = TBK // TS
NBC = N_CTX // TBK
NBL = N_LAT // TBK
NBK = NBC + NBL
NT = N_CTX + N_LAT
W_WIN = W_LRU + W_POOL + W_CONF
MIX_LAG = 2
OUT_LAG = 3
VMEM_LIMIT = 48 * 1024 * 1024

f32 = jnp.float32
bf16 = jnp.bfloat16


def _silu(g):
    return g * jax.nn.sigmoid(g)


def _block_flags(c):
    is_lat = c >= NBC
    first = jnp.logical_or(c == 0, c == NBC)
    last = jnp.logical_or(c == NBC - 1, c == NBK - 1)
    return is_lat, first, last


def _clamp_block(c):
    return jnp.clip(c, 0, NBK - 1)


def _mod_rows(mod_ref, is_lat):
    mod = mod_ref[is_lat.astype(jnp.int32)]
    return mod[:, 0:D], mod[:, D:2 * D], mod[:, 2 * D:3 * D]


def _rows(sb):
    if isinstance(sb, int):
        return slice(sb * TS * NB, (sb + 1) * TS * NB)
    return pl.ds(pl.multiple_of(sb * (TS * NB), TS * NB), TS * NB)


def _steps(sb):
    if isinstance(sb, int):
        return slice(sb * TS, (sb + 1) * TS)
    return pl.ds(pl.multiple_of(sb * TS, TS), TS)


def _tile(n):
    return slice(256 * n, 256 * (n + 1))


def _norm_piece(x_ref, gs, sh, hx_ref, sb):
    x = x_ref[_steps(sb)]
    ms = jnp.mean(x * x, axis=-1, keepdims=True)
    hx = (x * lax.rsqrt(ms + EPS)) * gs + sh
    hx_ref[_rows(sb), :] = hx.reshape(TS * NB, D).astype(bf16)


def _softplus(y):
    return jnp.maximum(y, 0.0) + jnp.log1p(jnp.exp(-jnp.abs(y)))


def _dwconv_piece(win_ref, o, cols, taps_ref, bias_ref, n_taps):
    acc = [None] * TS
    for tau in range(TS + n_taps - 1):
        zt = win_ref[o + tau, :, cols]
        for t in range(max(0, tau - n_taps + 1), min(TS, tau + 1)):
            term = zt * taps_ref[tau - t]
            acc[t] = term if acc[t] is None else acc[t] + term
    return jnp.stack(acc) + bias_ref[...][None]


def _conv4_piece(win_ref, off, lcw_ref, lcb_ref, xc_ref, sb):
    o = off + sb * TS - LRU_K // 2
    xc_ref[_steps(sb)] = _dwconv_piece(win_ref, o, slice(0, W_LRU), lcw_ref, lcb_ref, LRU_K)


def _coef_dots(xc_ref, wa_ref, kt, p_ref):
    xb = xc_ref[:, :, _tile(kt)].reshape(RBK, 256).astype(bf16)
    p_ref[:, _tile(kt)] = jnp.dot(xb, wa_ref[0, kt], preferred_element_type=f32)
    p_ref[:, _tile(2 + kt)] = jnp.dot(xb, wa_ref[1, kt], preferred_element_type=f32)


def _coef_piece(p_ref, ba_ref, bx_ref, c8, kt, sb, a_ref, xc_ref):
    x2 = xc_ref[_steps(sb), :, _tile(kt)].reshape(TS * NB, 256)
    r = jax.nn.sigmoid(p_ref[_rows(sb), _tile(kt)] + ba_ref[:, _tile(kt)])
    ig = jax.nn.sigmoid(p_ref[_rows(sb), _tile(2 + kt)] + bx_ref[:, _tile(kt)])
    la = r * c8[:, _tile(kt)]
    a = jnp.exp(la)
    m2 = -jnp.tanh(la) * (1.0 + a * a)
    mult = jnp.where(m2 > 0.0, m2 * lax.rsqrt(m2), 0.0)
    b = mult * (ig * x2)
    a_ref[_steps(sb), :, _tile(kt)] = a.reshape(TS, NB, 256)
    xc_ref[_steps(sb), :, _tile(kt)] = b.reshape(TS, NB, 256)


def _rev_block(s):
    s = jnp.clip(s, 0, NBK - 1)
    return jnp.where(s < NBC, NBC - 1 - s, NBK - 1 - (s - NBC))


def _bwd_kernel(xi_ref, mod_ref, gpre_ref, w_ref, lcw_ref, lcb_ref, wa_ref, ba_ref, bx_ref, lam_ref,
                hb_ref, hx_scr, new_scr, mid_scr, win_scr, xc_scr, a_scr, p_scr, h_scr):
    s = pl.program_id(0)
    lat_i, _, _ = _block_flags(_rev_block(s))
    _, first_m, last_m = _block_flags(_rev_block(s - 1))

    @pl.when(s == 0)
    def _():
        win_scr[...] = jnp.zeros_like(win_scr)
        mid_scr[...] = jnp.zeros_like(mid_scr)
        h_scr[...] = jnp.zeros_like(h_scr)

    sh, sc, _ = _mod_rows(mod_ref, lat_i)
    gs = (gpre_ref[...] * (1.0 + sc))[None]
    sh = sh[None]
    c8 = -LRU_C * _softplus(-lam_ref[...])

    for sb in range(NSB):
        _conv4_piece(win_scr, HT, lcw_ref, lcb_ref, xc_scr, sb)
    for kt in range(2):
        _coef_dots(xc_scr, wa_ref, kt, p_scr)
    for sb in range(NSB):
        _norm_piece(xi_ref, gs, sh, hx_scr, sb)
        _coef_piece(p_scr, ba_ref, bx_ref, c8, 0, sb, a_scr, xc_scr)
    for n in range(2):
        v = jnp.dot(hx_scr[...], w_ref[:, _tile(n)], preferred_element_type=f32)
        new_scr[:, :, _tile(n)] = v.reshape(TBK, NB, 256)
        for sb in range(n * NSB // 2, (n + 1) * NSB // 2):
            _coef_piece(p_scr, ba_ref, bx_ref, c8, 1, sb, a_scr, xc_scr)
    h = jnp.where(s == 2, 0.0, h_scr[...])
    for t in range(TBK - 1, -1, -1):
        h = a_scr[t] * h + xc_scr[t]
        hb_ref[t] = h
    h_scr[...] = h

    win_scr[HT + TBK:HT + TBK + HH] = jnp.where(last_m, 0.0, win_scr[HT:HT + HH])
    win_scr[HT:HT + TBK] = mid_scr[...]
    win_scr[0:HT] = jnp.where(first_m, 0.0, new_scr[TBK - HT:TBK])
    mid_scr[...] = new_scr[...]


def _fwd_kernel(xi_ref, xo_ref, hb_ref, mod_ref, gpre_ref, w_ref, lcw_ref, lcb_ref, wa_ref, ba_ref,
                bx_ref, lam_ref, poolw_ref, pscale_ref, cdw_ref, cdb_ref, lng_ref, lnb_ref, pw_ref,
                wout_ref, gpost_ref, out_ref,
                hx_new, hx_mid, hx_mix, new_scr, mid_scr, win_scr, xc_scr, a_scr, p_scr, g_scr, ov_scr,
                d_scr, c_scr, z_new, z_prev, h_scr):
    i = pl.program_id(0)
    lat_i, _, _ = _block_flags(_clamp_block(i))
    _, first_m, last_m = _block_flags(_clamp_block(i - 1))
    bj = _clamp_block(i - MIX_LAG)
    lat_j, _, _ = _block_flags(bj)
    lat_o, _, _ = _block_flags(_clamp_block(i - OUT_LAG))

    @pl.when(i == 0)
    def _():
        win_scr[...] = jnp.zeros_like(win_scr)
        mid_scr[...] = jnp.zeros_like(mid_scr)
        hx_mid[...] = jnp.zeros_like(hx_mid)
        hx_mix[...] = jnp.zeros_like(hx_mix)
        z_prev[...] = jnp.zeros_like(z_prev)
        h_scr[...] = jnp.zeros_like(h_scr)

    sh, sc, _ = _mod_rows(mod_ref, lat_i)
    gs = (gpre_ref[...] * (1.0 + sc))[None]
    sh = sh[None]
    _, _, gt = _mod_rows(mod_ref, lat_o)
    gg = gt * gpost_ref[...]
    c8 = -LRU_C * _softplus(-lam_ref[...])

    def conf_piece(sb):
        o = HW + sb * TS - CONF_K // 2
        acc = _dwconv_piece(win_scr, o, slice(W_LRU + W_POOL, W_WIN), cdw_ref, cdb_ref, CONF_K)
        mu = jnp.mean(acc, axis=-1, keepdims=True)
        dz = acc - mu
        var = jnp.mean(dz * dz, axis=-1, keepdims=True)
        zn = dz * lax.rsqrt(var + LN_EPS) * lng_ref[...][None] + lnb_ref[...][None]
        c_scr[_rows(sb), :] = _silu(zn).reshape(TS * NB, W_CONF).astype(bf16)

    nseq = jnp.where(lat_j, N_LAT, N_CTX)
    tseq0 = (bj - jnp.where(lat_j, NBC, 0)) * TBK
    hi_half = lax.broadcasted_iota(jnp.int32, (1, 1, 128), 2) >= 64
    hw_a = jnp.where(hi_half, 2, 1)
    hw_b = jnp.where(hi_half, 8, 4)

    def pool_piece(sb):
        ca, cb = W_LRU, W_LRU + 128
        base = HW + sb * TS
        tg = lax.broadcasted_iota(jnp.int32, (TS, NB, 128), 0) + (tseq0 + sb * TS)

        def count(hw):
            return (jnp.minimum(tg + hw, nseq) - jnp.maximum(tg - hw, 0)).astype(f32)

        def ld(k, c0):
            return win_scr[base + k:base + k + TS, :, c0:c0 + 128]

        va = ld(0, ca)
        sa = (ld(-1, ca) + va) + jnp.where(hi_half, ld(-2, ca) + ld(1, ca), 0.0)
        vb = ld(0, cb)
        inner = vb
        for k in (-4, -3, -2, -1, 1, 2, 3):
            inner = inner + ld(k, cb)
        outer = ld(-8, cb)
        for k in (-7, -6, -5, 4, 5, 6, 7):
            outer = outer + ld(k, cb)
        sbv = inner + jnp.where(hi_half, outer, 0.0)
        da = sa / count(hw_a) - va
        db = sbv / count(hw_b) - vb
        d_scr[_rows(sb), 0:128] = da.reshape(TS * NB, 128).astype(bf16)
        d_scr[_rows(sb), 128:256] = db.reshape(TS * NB, 128).astype(bf16)

    def spread(pieces, m, n):
        return pieces[n * len(pieces) // m:(n + 1) * len(pieces) // m]

    for n in range(4):
        ov_scr[n] = jnp.dot(z_prev[...], wout_ref[n], preferred_element_type=f32)
        for sb in spread(list(range(NSB)), 4, n):
            conf_piece(sb)

    for n in range(4):
        g_scr[:, _tile(n)] = jnp.dot(hx_mix[...], w_ref[:, N_VAL + 256 * n:N_VAL + 256 * (n + 1)],
                                     preferred_element_type=f32)
        for sb in spread(list(range(NSB)), 4, n):
            _norm_piece(xi_ref, gs, sh, hx_new, sb)
            _conv4_piece(win_scr, HW, lcw_ref, lcb_ref, xc_scr, sb)

    for kt in range(2):
        _coef_dots(xc_scr, wa_ref, kt, p_scr)

    def value_tile(n):
        return jnp.dot(hx_new[...], w_ref[:, _tile(n)], preferred_element_type=f32)

    def post_piece(sb):
        obs = [ov_scr[n, _rows(sb), :] for n in range(4)]
        ssq = obs[0] * obs[0]
        for ob in obs[1:]:
            ssq = ssq + ob * ob
        rs = lax.rsqrt(jnp.sum(ssq, axis=-1, keepdims=True) * (1.0 / D) + EPS)
        for n, ob in enumerate(obs):
            y = (ob * rs).reshape(TS, NB, 256)
            out_ref[_steps(sb), :, _tile(n)] = xo_ref[_steps(sb), :, _tile(n)] + gg[:, _tile(n)][None] * y

    c_pieces = []
    for sb in range(NSB):
        c_pieces.append(lambda sb=sb: post_piece(sb))
        for kt in range(2):
            c_pieces.append(lambda sb=sb, kt=kt: _coef_piece(p_scr, ba_ref, bx_ref, c8, kt, sb, a_scr, xc_scr))
    vc = None
    for n in range(5):
        if n < 3:
            new_scr[:, :, _tile(n)] = value_tile(n).reshape(TBK, NB, 256)
        elif n == 3:
            vc = value_tile(3)
        else:
            zc = vc * jax.nn.sigmoid(value_tile(4))
            new_scr[:, :, _tile(3)] = zc.reshape(TBK, NB, W_CONF)
        for piece in spread(c_pieces, 5, n):
            piece()

    h = jnp.where(i == MIX_LAG, 0.0, h_scr[...])
    for sb in range(NSB):
        for t in range(sb * TS, (sb + 1) * TS):
            h = a_scr[t] * h + xc_scr[t]
            xc_scr[t] = h
        pool_piece(sb)
        y = xc_scr[_steps(sb)] + hb_ref[_steps(sb)]
        for n in range(2):
            zz = y[:, :, _tile(n)].reshape(TS * NB, 256) * _silu(g_scr[_rows(sb), _tile(n)])
            z_new[_rows(sb), _tile(n)] = zz.astype(bf16)
    h_scr[...] = h

    yp = jnp.dot(d_scr[...], poolw_ref[...], preferred_element_type=f32) * pscale_ref[...]
    yc = jnp.dot(c_scr[...], pw_ref[...], preferred_element_type=f32)
    for sb in range(NSB):
        z_new[_rows(sb), _tile(2)] = (yp[_rows(sb)] * _silu(g_scr[_rows(sb), _tile(2)])).astype(bf16)
        z_new[_rows(sb), _tile(3)] = (yc[_rows(sb)] * _silu(g_scr[_rows(sb), _tile(3)])).astype(bf16)

    win_scr[0:HW] = jnp.where(first_m, 0.0, win_scr[TBK:TBK + HW])
    win_scr[HW:HW + TBK] = mid_scr[...]
    win_scr[HW + TBK:HW + TBK + HW] = jnp.where(last_m, 0.0, new_scr[0:HW])
    mid_scr[...] = new_scr[...]
    hx_mix[...] = hx_mid[...]
    hx_mid[...] = hx_new[...]
    z_prev[...] = z_new[...]


def _mod_kernel(cond_ref, w_ref, b_ref, out_ref):
    s = _silu(cond_ref[...]).astype(bf16)
    out_ref[...] = jnp.dot(s, w_ref[...].astype(bf16), preferred_element_type=f32) + b_ref[...]


TP = 256
RCH = 64


def _ingest_kernel(ctx_ref, x_ref, pos_ref, out_ref):
    i = pl.program_id(0)

    @pl.when(i == 0)
    def _():
        for b in range(NB):
            for r in range(0, TP, RCH):
                out_ref[r:r + RCH, b, :] = ctx_ref[b, r:r + RCH, :]

    @pl.when(i > 0)
    def _():
        for b in range(NB):
            for r in range(0, TP, RCH):
                out_ref[r:r + RCH, b, :] = x_ref[b, r:r + RCH, :] + pos_ref[r:r + RCH, :]


def _egress_kernel(s_ref, out_ref):
    for b in range(NB):
        for r in range(0, TP, RCH):
            out_ref[b, r:r + RCH, :] = s_ref[r:r + RCH, b, :]


def _pos_embed(n):
    rows = n // GRID_W
    q = D // 4
    omega = 1.0 / (10000.0 ** (jnp.arange(q, dtype=f32) / q))

    def enc(p):
        ang = p[:, None] * omega[None, :]
        return jnp.concatenate([jnp.sin(ang), jnp.cos(ang)], axis=-1)
    enc_row = jnp.repeat(enc(jnp.arange(rows, dtype=f32)), GRID_W, axis=0)
    enc_col = jnp.tile(enc(jnp.arange(GRID_W, dtype=f32)), (rows, 1))
    return jnp.concatenate([enc_row, enc_col], axis=-1)


def _block_diag(w, per_tile):
    *lead, g, n, _ = w.shape
    w = w.reshape(*lead, g // per_tile, per_tile, n, n)
    eye = jnp.eye(per_tile, dtype=w.dtype)
    out = jnp.einsum('...jio,jm->...jimo', w, eye)
    return out.reshape(*lead, g // per_tile, per_tile * n, per_tile * n)


def _const_spec(block, index):
    return pl.BlockSpec(block, lambda i: index, pipeline_mode=pl.Buffered(1))


def kernel(x, c, ctx, c_ctx, w_mod, b_mod, g_pre, w_in, lru_conv_w, lru_conv_b, lru_wa, lru_ba, lru_wx,
           lru_bx, lru_lambda, pool_w, pool_scale, conf_dw, conf_dw_b, conf_ln_g, conf_ln_b, conf_pw,
           w_out, g_post):
    L = DEPTH
    cols = [(0, 512), (1024, 1280), (1536, 2048), (512, 1024), (1280, 1536), (2048, 2304)]
    w_all = jnp.concatenate([w_in[:, :, a:b] for a, b in cols], axis=-1).astype(bf16)
    wa_t = jnp.stack([_block_diag(lru_wa, 4), _block_diag(lru_wx, 4)], axis=2).astype(bf16)
    poolw_t = _block_diag(pool_w, 4)[:, 0].astype(bf16)
    pw_b = conf_pw.astype(bf16)
    wout_b = w_out.astype(bf16).reshape(L, D, 4, 256).transpose(0, 2, 1, 3)
    row = lambda a: a[:, None, :]
    dir_row = lambda a: a[:, :, None, :]
    taps = lambda a: jnp.broadcast_to(a[:, :, None, :], (*a.shape[:2], NB, a.shape[2]))
    lcw_t = taps(lru_conv_w)
    cdw_t = taps(conf_dw)

    cond = jnp.concatenate([jnp.broadcast_to(c_ctx[None, :], (NB, D)), c], axis=0)
    tn = 1024
    mod = pl.pallas_call(
        _mod_kernel,
        out_shape=jax.ShapeDtypeStruct((L, 2 * NB, 3 * D), f32),
        grid=(L, 3 * D // tn),
        in_specs=[pl.BlockSpec((2 * NB, D), lambda l, n: (0, 0)),
                  pl.BlockSpec((None, D, tn), lambda l, n: (l, 0, n)),
                  pl.BlockSpec((None, 1, tn), lambda l, n: (l, 0, n))],
        out_specs=pl.BlockSpec((None, 2 * NB, tn), lambda l, n: (l, 0, n)),
        name="modulation",
    )(cond, w_mod, row(b_mod))
    mod = mod.reshape(L, 2, NB, 3 * D)

    assert N_CTX == TP
    s = pl.pallas_call(
        _ingest_kernel,
        out_shape=jax.ShapeDtypeStruct((NT, NB, D), f32),
        grid=(NT // TP,),
        in_specs=[pl.BlockSpec((NB, TP, D), lambda i: (0, 0, 0)),
                  pl.BlockSpec((NB, TP, D), lambda i: (0, jnp.maximum(i - 1, 0), 0)),
                  pl.BlockSpec((TP, D), lambda i: (jnp.maximum(i - 1, 0), 0))],
        out_specs=pl.BlockSpec((TP, NB, D), lambda i: (i, 0, 0)),
        compiler_params=pltpu.CompilerParams(dimension_semantics=("arbitrary",), vmem_limit_bytes=VMEM_LIMIT),
        name="ingest",
    )(ctx, x, _pos_embed(N_LAT))

    cparams = pltpu.CompilerParams(dimension_semantics=("arbitrary",), vmem_limit_bytes=VMEM_LIMIT)

    for l in range(L):
        def lru_specs(d):
            return [
                _const_spec((None, LRU_K, NB, W_LRU), (l, 0, 0, 0)),
                _const_spec((None, 1, W_LRU), (l, 0, 0)),
                _const_spec((None, None, 2, 2, 256, 256), (l, d, 0, 0, 0, 0)),
                _const_spec((None, None, 1, W_LRU), (l, d, 0, 0)),
                _const_spec((None, None, 1, W_LRU), (l, d, 0, 0)),
                _const_spec((None, None, 1, W_LRU), (l, d, 0, 0)),
            ]
        lru_args = (lcw_t, row(lru_conv_b), wa_t, dir_row(lru_ba), dir_row(lru_bx), dir_row(lru_lambda))
        mod_specs = [_const_spec((None, 2, NB, 3 * D), (l, 0, 0, 0)), _const_spec((None, 1, D), (l, 0, 0))]

        hb = pl.pallas_call(
            _bwd_kernel,
            out_shape=jax.ShapeDtypeStruct((NT, NB, W_LRU), f32),
            grid=(NBK + 2,),
            in_specs=[pl.BlockSpec((TBK, NB, D), lambda s_: (_rev_block(s_), 0, 0)),
                      *mod_specs,
                      _const_spec((None, D, W_LRU), (l, 0, 0)),
                      *lru_specs(1)],
            out_specs=pl.BlockSpec((TBK, NB, W_LRU), lambda s_: (_rev_block(s_ - 2), 0, 0)),
            scratch_shapes=[
                pltpu.VMEM((RBK, D), bf16),
                pltpu.VMEM((TBK, NB, W_LRU), f32),
                pltpu.VMEM((TBK, NB, W_LRU), f32),
                pltpu.VMEM((HT + TBK + HH, NB, W_LRU), f32),
                pltpu.VMEM((TBK, NB, W_LRU), f32),
                pltpu.VMEM((TBK, NB, W_LRU), f32),
                pltpu.VMEM((RBK, 2 * W_LRU), f32),
                pltpu.VMEM((NB, W_LRU), f32),
            ],
            compiler_params=cparams,
            name=f"lru_bwd_{l}",
        )(s, mod, row(g_pre), w_all, *lru_args)

        s = pl.pallas_call(
            _fwd_kernel,
            out_shape=jax.ShapeDtypeStruct((NT, NB, D), f32),
            grid=(NBK + OUT_LAG,),
            in_specs=[pl.BlockSpec((TBK, NB, D), lambda i: (_clamp_block(i), 0, 0)),
                      pl.BlockSpec((TBK, NB, D), lambda i: (_clamp_block(i - OUT_LAG), 0, 0)),
                      pl.BlockSpec((TBK, NB, W_LRU), lambda i: (_clamp_block(i - MIX_LAG), 0, 0)),
                      *mod_specs,
                      _const_spec((None, D, N_IN), (l, 0, 0)),
                      *lru_specs(0),
                      _const_spec((None, W_POOL, W_POOL), (l, 0, 0)),
                      _const_spec((None, 1, W_POOL), (l, 0, 0)),
                      _const_spec((None, CONF_K, NB, W_CONF), (l, 0, 0, 0)),
                      _const_spec((None, 1, W_CONF), (l, 0, 0)),
                      _const_spec((None, 1, W_CONF), (l, 0, 0)),
                      _const_spec((None, 1, W_CONF), (l, 0, 0)),
                      _const_spec((None, W_CONF, W_CONF), (l, 0, 0)),
                      _const_spec((None, 4, D, 256), (l, 0, 0, 0)),
                      _const_spec((None, 1, D), (l, 0, 0))],
            out_specs=pl.BlockSpec((TBK, NB, D), lambda i: (_clamp_block(i - OUT_LAG), 0, 0)),
            scratch_shapes=[
                pltpu.VMEM((RBK, D), bf16),
                pltpu.VMEM((RBK, D), bf16),
                pltpu.VMEM((RBK, D), bf16),
                pltpu.VMEM((TBK, NB, W_WIN), f32),
                pltpu.VMEM((TBK, NB, W_WIN), f32),
                pltpu.VMEM((HW + TBK + HW, NB, W_WIN), f32),
                pltpu.VMEM((TBK, NB, W_LRU), f32),
                pltpu.VMEM((TBK, NB, W_LRU), f32),
                pltpu.VMEM((RBK, 2 * W_LRU), f32),
                pltpu.VMEM((RBK, D), f32),
                pltpu.VMEM((4, RBK, 256), f32),
                pltpu.VMEM((RBK, W_POOL), bf16),
                pltpu.VMEM((RBK, W_CONF), bf16),
                pltpu.VMEM((RBK, D), bf16),
                pltpu.VMEM((RBK, D), bf16),
                pltpu.VMEM((NB, W_LRU), f32),
            ],
            compiler_params=cparams,
            name=f"layer_fwd_{l}",
        )(s, s, hb, mod, row(g_pre), w_all, *lru_args, poolw_t, row(pool_scale), cdw_t, row(conf_dw_b),
          row(conf_ln_g), row(conf_ln_b), pw_b, wout_b, row(g_post))

    return pl.pallas_call(
        _egress_kernel,
        out_shape=jax.ShapeDtypeStruct((NB, N_LAT, D), f32),
        grid=(N_LAT // TP,),
        in_specs=[pl.BlockSpec((TP, NB, D), lambda i: (i + N_CTX // TP, 0, 0))],
        out_specs=pl.BlockSpec((NB, TP, D), lambda i: (0, i, 0)),
        compiler_params=pltpu.CompilerParams(dimension_semantics=("arbitrary",), vmem_limit_bytes=VMEM_LIMIT),
        name="egress",
    )(s)
```
